```python
import math, functools
import jax, jax.numpy as jnp
from jax import lax
import numpy as np

D_MODEL = 4096
BATCH = 4
SEQ = 2048
DEPTH = 2
DEC_BATCH = 8
DEC_SEQ = 4
PAST_LEN = 16384
PAGE_SIZE = 128

D_MIX = D_MODEL
GROUP_W = D_MIX // 4
CONV_W = GROUP_W
CONV_K = 3
H_B = 4
DV_B = GROUP_W // H_B
DK_B = DV_B // 2
H_C = 4
DV_C = GROUP_W // H_C
DH_C = DV_C // 2
H_D = 4
DV_D = GROUP_W // H_D
DK_D = DV_D // 2
H_X = 4
DH_X = D_MODEL // 16
X_W = H_X * DH_X
N_MEM = 256
D_FF = 4 * D_MODEL
CHUNK = 64
QBLK = 128
EPS = 1e-6
IN_SIZES = (CONV_W, CONV_W, CONV_W,
            H_B * DK_B, H_B * DK_B, H_B * DV_B, H_B * DV_B, H_B, H_B,
            2 * H_C * DH_C, 2 * H_C * DH_C, H_C * DV_C,
            H_D * DK_D, H_D * DK_D, H_D * DV_D, H_D * DV_D)
N_IN = sum(IN_SIZES)

kernel_name = 'hybrid_parallel_groups_decode_step'


def rms_norm(x, g):
    xf = x.astype(jnp.float32)
    y = xf * lax.rsqrt(jnp.mean(xf * xf, axis=-1, keepdims=True) + EPS)
    return (y * g.astype(jnp.float32)).astype(x.dtype)


def split_projection(z):
    offs = np.cumsum(IN_SIZES)[:-1].tolist()
    return jnp.split(z, offs, axis=-1)


def alibi_slopes():
    return 2.0 ** (-8.0 * jnp.arange(1, H_C + 1, dtype=jnp.float32) / H_C)


def retention_log_decay():
    return jnp.log1p(-(2.0 ** (-5.0 - jnp.arange(H_D, dtype=jnp.float32))))


def chunk_len(T):
    return CHUNK if T % CHUNK == 0 else T


def short_conv(u, buf, w):
    T = u.shape[1]
    ext = jnp.concatenate([buf.astype(u.dtype), u], axis=1)
    y = w[0] * ext[:, 0:T] + w[1] * ext[:, 1:T + 1] + w[2] * ext[:, 2:T + 2]
    return y, ext[:, T:]


def _mlstm_step(carry, inp):
    C0, n0, m0 = carry
    q, k, v, ig, lf = inp
    L = q.shape[2]
    b = jnp.cumsum(lf, axis=-1)
    causal = jnp.tril(jnp.ones((L, L), dtype=bool))
    dmat = jnp.where(causal, b[..., :, None] - b[..., None, :] + ig[..., None, :], -jnp.inf)
    inter = b + m0[..., None]
    m = jnp.maximum(inter, jnp.max(dmat, axis=-1))
    w_inter = jnp.exp(inter - m)
    w_intra = jnp.exp(dmat - m[..., None]) * jnp.einsum('bhtd,bhsd->bhts', q, k)
    num = w_inter[..., None] * jnp.einsum('bhtd,bhde->bhte', q, C0) + jnp.einsum('bhts,bhse->bhte', w_intra, v)
    den = w_inter * jnp.einsum('bhtd,bhd->bht', q, n0) + jnp.sum(w_intra, axis=-1)
    h = num / jnp.maximum(jnp.abs(den), jnp.exp(-m))[..., None]
    m_end = m[..., -1]
    g_state = jnp.exp(b[..., -1] + m0 - m_end)
    g_tok = jnp.exp(b[..., -1:] - b + ig - m_end[..., None])
    C = g_state[..., None, None] * C0 + jnp.einsum('bhs,bhsd,bhse->bhde', g_tok, k, v)
    n = g_state[..., None] * n0 + jnp.einsum('bhs,bhsd->bhd', g_tok, k)
    return (C, n, m_end), h


def mlstm(q, k, v, ig, lf, C0, n0, m0):
    f32 = jnp.float32
    B, H, T, dk = q.shape
    L = chunk_len(T)
    nc = T // L

    def chunks(a):
        a = a.astype(f32)
        return jnp.moveaxis(a.reshape((B, H, nc, L) + a.shape[3:]), 2, 0)

    inp = (chunks(q), chunks(k) * dk ** -0.5, chunks(v), chunks(ig), chunks(lf))
    (C, n, m), hs = lax.scan(_mlstm_step, (C0.astype(f32), n0.astype(f32), m0.astype(f32)), inp)
    return jnp.moveaxis(hs, 0, 2).reshape(B, H, T, -1), C, n, m


def _ret_step(S0, inp, log_gamma):
    q, k, v = inp
    L = q.shape[2]
    idx = jnp.arange(L, dtype=jnp.float32)
    dist = idx[:, None] - idx[None, :]
    decay = jnp.where(dist >= 0, jnp.exp(jnp.maximum(dist, 0.0)[None] * log_gamma[:, None, None]), 0.0)
    s = jnp.einsum('bhtd,bhsd->bhts', q, k) * decay
    inter = jnp.exp((idx + 1.0)[None, :] * log_gamma[:, None])
    o = jnp.einsum('bhts,bhse->bhte', s, v) + inter[:, :, None] * jnp.einsum('bhtd,bhde->bhte', q, S0)
    tail = jnp.exp((L - 1.0 - idx)[None, :] * log_gamma[:, None])
    S = jnp.exp(L * log_gamma)[:, None, None] * S0 + jnp.einsum('hs,bhsd,bhse->bhde', tail, k, v)
    return S, o


def retention(q, k, v, S0, log_gamma):
    f32 = jnp.float32
    B, H, T, dk = q.shape
    L = chunk_len(T)
    nc = T // L

    def chunks(a):
        a = a.astype(f32)
        return jnp.moveaxis(a.reshape((B, H, nc, L) + a.shape[3:]), 2, 0)

    inp = (chunks(q), chunks(k) * dk ** -0.5, chunks(v))
    S, os_ = lax.scan(functools.partial(_ret_step, log_gamma=log_gamma), S0.astype(f32), inp)
    return jnp.moveaxis(os_, 0, 2).reshape(B, H, T, -1), S


def _diff_block(q, qpos, k, v, kpos, lam, slopes):
    dist = qpos[:, None] - kpos[None, :]
    s = jnp.einsum('bqhjd,bkhjd->jbhqk', q, k).astype(jnp.float32) * DH_C ** -0.5
    s = jnp.where(dist >= 0, s - slopes[:, None, None] * dist.astype(jnp.float32), -jnp.inf)
    p = jax.nn.softmax(s, axis=-1)
    a = p[0] - lam * p[1]
    return jnp.einsum('bhqk,bkhe->bqhe', a.astype(v.dtype), v)


def diff_attention(q, k, v, q0, lam, slopes):
    B, T = q.shape[:2]
    nb = T // QBLK if T % QBLK == 0 else 1
    bq = T // nb
    kpos = jnp.arange(k.shape[1])
    qb = jnp.moveaxis(q.reshape((B, nb, bq) + q.shape[2:]), 1, 0)
    starts = q0 + bq * jnp.arange(nb)
    out = lax.map(lambda blk: _diff_block(blk[0], blk[1] + jnp.arange(bq), k, v, kpos, lam, slopes), (qb, starts))
    return jnp.moveaxis(out, 0, 1).reshape((B, T) + out.shape[3:])


def mixer_block(h, p, lam_init, slopes, log_gamma, conv_buf, C0, n0, m0, S0, k_past, v_past):
    f32 = jnp.float32
    B, T, _ = h.shape
    (a_x, a_b, a_c, q_b, k_b, v_b, o_b, i_b, f_b,
     q_c, k_c, v_c, q_d, k_d, v_d, g_d) = split_projection(h @ p['w_in'])

    def heads(t, H):
        return jnp.swapaxes(t.reshape(B, T, H, -1), 1, 2)

    conv_y, conv_new = short_conv(a_c * a_x, conv_buf, p['conv_w'])
    y_a = a_b * conv_y
    ig = jnp.swapaxes(i_b.astype(f32) + p['b_i'], 1, 2)
    lf = jax.nn.log_sigmoid(jnp.swapaxes(f_b.astype(f32) + p['b_f'], 1, 2))
    hb, C, n, m = mlstm(heads(q_b, H_B), heads(k_b, H_B), heads(v_b, H_B), ig, lf, C0, n0, m0)
    hb = rms_norm(jnp.swapaxes(hb, 1, 2), p['g_mlstm'].reshape(H_B, DV_B))
    y_b = (hb * jax.nn.sigmoid(o_b.astype(f32)).reshape(B, T, H_B, DV_B)).reshape(B, T, GROUP_W)
    qc = rms_norm(q_c.reshape(B, T, H_C, 2, DH_C), p['g_dq'])
    kc = rms_norm(k_c.reshape(B, T, H_C, 2, DH_C), p['g_dk'])
    vc = v_c.reshape(B, T, H_C, DV_C)
    lam = (jnp.exp(jnp.sum(p['lq1'].astype(f32) * p['lk1'].astype(f32)))
           - jnp.exp(jnp.sum(p['lq2'].astype(f32) * p['lk2'].astype(f32))) + lam_init)
    if k_past is None:
        k_all, v_all, q0 = kc, vc, 0
    else:
        k_all = jnp.concatenate([k_past.reshape(B, -1, H_C, 2, DH_C).astype(kc.dtype), kc], axis=1)
        v_all = jnp.concatenate([v_past.astype(vc.dtype), vc], axis=1)
        q0 = k_past.shape[1]
    oc = diff_attention(qc, k_all, v_all, q0, lam, slopes)
    y_c = (rms_norm(oc, p['g_dout'].reshape(H_C, DV_C)) * (1.0 - lam_init)).reshape(B, T, GROUP_W)
    od, S = retention(heads(q_d, H_D), heads(k_d, H_D), heads(v_d, H_D), S0, log_gamma)
    od = rms_norm(jnp.swapaxes(od, 1, 2), p['g_ret'].reshape(H_D, DV_D))
    y_d = od.reshape(B, T, GROUP_W) * jax.nn.silu(g_d)
    dt = h.dtype
    y = jnp.concatenate([y_a.astype(dt), y_b.astype(dt), y_c.astype(dt), y_d.astype(dt)], axis=-1) @ p['w_out']
    return y, conv_new, C, n, m, kc.reshape(B, T, H_C, 2 * DH_C), vc, S


def memory_kv(mem, p):
    B, M, _ = mem.shape
    hm = rms_norm(mem, p['norm_mem'])
    mk = rms_norm((hm @ p['w_xk']).reshape(B, M, H_X, DH_X), p['g_xk'])
    mv = (hm @ p['w_xv']).reshape(B, M, H_X, DH_X)
    return mk, mv


def cross_attention(h, mk, mv, p):
    B, T, _ = h.shape
    q = rms_norm((h @ p['w_xq']).reshape(B, T, H_X, DH_X), p['g_xq'])
    s = jnp.einsum('bthd,bmhd->bhtm', q, mk.astype(q.dtype)).astype(jnp.float32) * DH_X ** -0.5
    a = jax.nn.softmax(s, axis=-1).astype(mv.dtype)
    o = jnp.einsum('bhtm,bmhd->bthd', a, mv)
    return o.reshape(B, T, X_W).astype(h.dtype) @ p['w_xo']


def sq_relu_mlp(h, p):
    return jnp.square(jax.nn.relu(h @ p['w_up'])) @ p['w_down']


def setup_inputs(seed: int = 0) -> dict:
    key = jax.random.key(seed)
    ks = iter(jax.random.split(key, 48))
    f32 = jnp.float32

    def nrm(shape, scale):
        return scale * jax.random.normal(next(ks), shape, f32)

    def gain(shape):
        return 1.0 + nrm(shape, 0.02)

    n_pages = PAST_LEN // PAGE_SIZE
    n_pool = (DEC_BATCH * n_pages * 5) // 4
    x_prompt = nrm((BATCH, SEQ, D_MODEL), 1.0)
    x_sample = nrm((DEC_BATCH, DEC_SEQ, D_MODEL), 1.0)
    cache_diff_k = nrm((DEPTH, n_pool, PAGE_SIZE, H_C, 2 * DH_C), 1.0)
    cache_diff_v = nrm((DEPTH, n_pool, PAGE_SIZE, H_C, DV_C), 1.0)
    cache_mem_k = nrm((DEPTH, DEC_BATCH, N_MEM, H_X, DH_X), 1.0)
    cache_mem_v = nrm((DEPTH, DEC_BATCH, N_MEM, H_X, DH_X), 1.0)
    state_conv = nrm((DEPTH, DEC_BATCH, CONV_K - 1, CONV_W), 1.0)
    state_mlstm_C = nrm((DEPTH, DEC_BATCH, H_B, DK_B, DV_B), 0.1)
    state_mlstm_n = nrm((DEPTH, DEC_BATCH, H_B, DK_B), 0.1)
    state_mlstm_m = nrm((DEPTH, DEC_BATCH, H_B), 1.0)
    state_ret = nrm((DEPTH, DEC_BATCH, H_D, DK_D, DV_D), 0.3)
    page_table = jax.random.permutation(next(ks), n_pool)[:DEC_BATCH * n_pages].reshape(DEC_BATCH, n_pages).astype(jnp.int32)
    mem_prompt = nrm((BATCH, N_MEM, D_MODEL), 1.0)
    return {
        'x_prompt': x_prompt, 'x_sample': x_sample,
        'cache_diff_k': cache_diff_k, 'cache_diff_v': cache_diff_v,
        'cache_mem_k': cache_mem_k, 'cache_mem_v': cache_mem_v,
        'state_conv': state_conv, 'state_mlstm_C': state_mlstm_C,
        'state_mlstm_n': state_mlstm_n, 'state_mlstm_m': state_mlstm_m,
        'state_ret': state_ret, 'page_table': page_table, 'mem_prompt': mem_prompt,
        'norm_mix': gain((DEPTH, D_MODEL)),
        'w_in': nrm((DEPTH, D_MODEL, N_IN), D_MODEL ** -0.5),
        'conv_w': nrm((DEPTH, CONV_K, CONV_W), CONV_K ** -0.5),
        'b_igate': nrm((DEPTH, H_B), 0.1),
        'b_fgate': jnp.linspace(3.0, 6.0, H_B, dtype=f32)[None, :] + nrm((DEPTH, H_B), 0.1),
        'g_mlstm': gain((DEPTH, H_B * DV_B)),
        'g_diff_q': gain((DEPTH, DH_C)),
        'g_diff_k': gain((DEPTH, DH_C)),
        'lam_q1': nrm((DEPTH, DH_C), 0.1),
        'lam_k1': nrm((DEPTH, DH_C), 0.1),
        'lam_q2': nrm((DEPTH, DH_C), 0.1),
        'lam_k2': nrm((DEPTH, DH_C), 0.1),
        'g_diff_out': gain((DEPTH, H_C * DV_C)),
        'g_ret': gain((DEPTH, H_D * DV_D)),
        'w_out': nrm((DEPTH, D_MIX, D_MODEL), D_MIX ** -0.5),
        'norm_xattn': gain((DEPTH, D_MODEL)),
        'norm_mem': gain((DEPTH, D_MODEL)),
        'w_xq': nrm((DEPTH, D_MODEL, X_W), D_MODEL ** -0.5),
        'w_xk': nrm((DEPTH, D_MODEL, X_W), D_MODEL ** -0.5),
        'w_xv': nrm((DEPTH, D_MODEL, X_W), D_MODEL ** -0.5),
        'g_xq': gain((DEPTH, DH_X)),
        'g_xk': gain((DEPTH, DH_X)),
        'w_xo': nrm((DEPTH, X_W, D_MODEL), X_W ** -0.5),
        'norm_mlp': gain((DEPTH, D_MODEL)),
        'w_up': nrm((DEPTH, D_MODEL, D_FF), D_MODEL ** -0.5),
        'w_down': nrm((DEPTH, D_FF, D_MODEL), D_FF ** -0.5),
    }


def reference(x_prompt, x_sample, cache_diff_k, cache_diff_v, cache_mem_k, cache_mem_v,
              state_conv, state_mlstm_C, state_mlstm_n, state_mlstm_m, state_ret,
              page_table, mem_prompt,
              norm_mix, w_in, conv_w, b_igate, b_fgate, g_mlstm, g_diff_q, g_diff_k,
              lam_q1, lam_k1, lam_q2, lam_k2, g_diff_out, g_ret, w_out,
              norm_xattn, norm_mem, w_xq, w_xk, w_xv, g_xq, g_xk, w_xo,
              norm_mlp, w_up, w_down):
    f32 = jnp.float32
    slopes = alibi_slopes()
    log_gamma = retention_log_decay()
    B = x_prompt.shape[0]
    DB = x_sample.shape[0]
    n_pages = page_table.shape[1]
    past = n_pages * PAGE_SIZE
    x_p, x_s = x_prompt, x_sample
    pc, pC, pn, pm, pk, pv, pS, pmk, pmv = [], [], [], [], [], [], [], [], []
    sc, sC, sn, sm, sk, sv, sS = [], [], [], [], [], [], []
    for li in range(DEPTH):
        p = dict(w_in=w_in[li], conv_w=conv_w[li], b_i=b_igate[li], b_f=b_fgate[li],
                 g_mlstm=g_mlstm[li], g_dq=g_diff_q[li], g_dk=g_diff_k[li],
                 lq1=lam_q1[li], lk1=lam_k1[li], lq2=lam_q2[li], lk2=lam_k2[li],
                 g_dout=g_diff_out[li], g_ret=g_ret[li], w_out=w_out[li],
                 norm_mem=norm_mem[li], w_xq=w_xq[li], w_xk=w_xk[li], w_xv=w_xv[li],
                 g_xq=g_xq[li], g_xk=g_xk[li], w_xo=w_xo[li], w_up=w_up[li], w_down=w_down[li])
        lam_init = 0.8 - 0.6 * math.exp(-0.3 * li)
        hp = rms_norm(x_p, norm_mix[li])
        yp, cv, C, n, m, kr, vr, S = mixer_block(
            hp, p, lam_init, slopes, log_gamma,
            jnp.zeros((B, CONV_K - 1, CONV_W), x_p.dtype),
            jnp.zeros((B, H_B, DK_B, DV_B), f32), jnp.zeros((B, H_B, DK_B), f32),
            jnp.zeros((B, H_B), f32), jnp.zeros((B, H_D, DK_D, DV_D), f32), None, None)
        x_p = x_p + yp
        mk, mv = memory_kv(mem_prompt, p)
        x_p = x_p + cross_attention(rms_norm(x_p, norm_xattn[li]), mk, mv, p)
        x_p = x_p + sq_relu_mlp(rms_norm(x_p, norm_mlp[li]), p)
        pc.append(cv); pC.append(C); pn.append(n); pm.append(m)
        pk.append(kr); pv.append(vr); pS.append(S); pmk.append(mk); pmv.append(mv)
        k_past = cache_diff_k[li][page_table].reshape(DB, past, H_C, 2 * DH_C)
        v_past = cache_diff_v[li][page_table].reshape(DB, past, H_C, DV_C)
        hs = rms_norm(x_s, norm_mix[li])
        ys, cv, C, n, m, kr, vr, S = mixer_block(
            hs, p, lam_init, slopes, log_gamma, state_conv[li],
            state_mlstm_C[li], state_mlstm_n[li], state_mlstm_m[li], state_ret[li], k_past, v_past)
        x_s = x_s + ys
        x_s = x_s + cross_attention(rms_norm(x_s, norm_xattn[li]), cache_mem_k[li], cache_mem_v[li], p)
        x_s = x_s + sq_relu_mlp(rms_norm(x_s, norm_mlp[li]), p)
        sc.append(cv); sC.append(C); sn.append(n); sm.append(m)
        sk.append(kr); sv.append(vr); sS.append(S)
    y_prompt = x_p
    y_sample = x_s
    p_conv = jnp.stack(pc)
    p_mlstm_C = jnp.stack(pC)
    p_mlstm_n = jnp.stack(pn)
    p_mlstm_m = jnp.stack(pm)
    p_diff_k = jnp.stack(pk)
    p_diff_v = jnp.stack(pv)
    p_ret = jnp.stack(pS)
    p_mem_k = jnp.stack(pmk)
    p_mem_v = jnp.stack(pmv)
    s_conv = jnp.stack(sc)
    s_mlstm_C = jnp.stack(sC)
    s_mlstm_n = jnp.stack(sn)
    s_mlstm_m = jnp.stack(sm)
    s_diff_k = jnp.stack(sk)
    s_diff_v = jnp.stack(sv)
    s_ret = jnp.stack(sS)
    return (y_prompt, y_sample, p_conv, p_mlstm_C, p_mlstm_n, p_mlstm_m, p_diff_k, p_diff_v, p_ret, p_mem_k, p_mem_v,
            s_conv, s_mlstm_C, s_mlstm_n, s_mlstm_m, s_diff_k, s_diff_v, s_ret)
```

```python
import functools
import math

import jax
import jax.numpy as jnp
from jax import lax
from jax.experimental import pallas as pl
from jax.experimental.pallas import tpu as pltpu

V7X_LANES = 128
V7X_VMEM_BYTES = 64 * 1024 * 1024
VMEM_LIMIT_CAP = V7X_VMEM_BYTES - 6 * 1024 * 1024

EPS = 1e-6
NEG = -1e30
CHUNK_ROWS = 128
SAMPLE_ROWS = 16
PAGES_PER_STEP = 4

BF16 = jnp.bfloat16
F32 = jnp.float32


def _params(semantics, vmem_estimate):
    limit = int(min(max(vmem_estimate * 5 // 4 + (4 << 20), 32 << 20), VMEM_LIMIT_CAP))
    return pltpu.CompilerParams(dimension_semantics=semantics, vmem_limit_bytes=limit)


def _nt_dot(a, b):
    return lax.dot_general(a, b, (((1,), (1,)), ((), ())), preferred_element_type=F32)


def _pad_rows(x, rows):
    if x.shape[0] == rows:
        return x
    return jnp.concatenate([x, jnp.zeros((rows - x.shape[0],) + x.shape[1:], x.dtype)], axis=0)


def _group_rms(x, gain, width):
    parts = []
    for g in range(x.shape[1] // width):
        blk = x[:, g * width:(g + 1) * width]
        ms = jnp.mean(blk * blk, axis=-1, keepdims=True)
        parts.append(blk * lax.rsqrt(ms + EPS) * gain)
    return parts[0] if len(parts) == 1 else jnp.concatenate(parts, axis=1)


def _rmsnorm_kernel(x_ref, g_ref, o_ref):
    x = x_ref[...]
    ms = jnp.mean(x * x, axis=-1, keepdims=True)
    o_ref[...] = (x * lax.rsqrt(ms + EPS) * g_ref[...]).astype(o_ref.dtype)


def rmsnorm_rows(x, g, rows_blk):
    m, d = x.shape
    rows_blk = min(rows_blk, m)
    return pl.pallas_call(
        _rmsnorm_kernel,
        grid=(m // rows_blk,),
        in_specs=[pl.BlockSpec((rows_blk, d), lambda i: (i, 0)),
                  pl.BlockSpec((1, d), lambda i: (0, 0))],
        out_specs=pl.BlockSpec((rows_blk, d), lambda i: (i, 0)),
        out_shape=jax.ShapeDtypeStruct((m, d), BF16),
        compiler_params=_params(("parallel",), rows_blk * d * 12),
        name="rmsnorm",
    )(x, g.reshape(1, d))


def _mm_kernel(*refs, n_a, kt, epilogue, group):
    a_refs = refs[:n_a]
    w_refs = refs[n_a:2 * n_a]
    pos = 2 * n_a
    extra_ref = None
    if epilogue in ("residual", "headnorm"):
        extra_ref = refs[pos]
        pos += 1
    o_ref = refs[pos]
    acc_ref = refs[pos + 1] if kt > 1 else None

    part = jnp.dot(a_refs[0][...], w_refs[0][...], preferred_element_type=F32)
    for a_ref, w_ref in zip(a_refs[1:], w_refs[1:]):
        part = part + jnp.dot(a_ref[...], w_ref[...], preferred_element_type=F32)

    def finish(acc):
        if epilogue == "residual":
            o_ref[...] = extra_ref[...] + acc
        elif epilogue == "relu2":
            o_ref[...] = jnp.square(jnp.maximum(acc, 0.0)).astype(o_ref.dtype)
        elif epilogue == "headnorm":
            o_ref[...] = _group_rms(acc, extra_ref[...], group).astype(o_ref.dtype)
        else:
            o_ref[...] = acc.astype(o_ref.dtype)

    if kt == 1:
        finish(part)
    else:
        k = pl.program_id(2)

        @pl.when(k == 0)
        def _():
            acc_ref[...] = part

        @pl.when(k > 0)
        def _():
            acc_ref[...] += part

        @pl.when(k == kt - 1)
        def _():
            finish(acc_ref[...])


def matmul(a_list, w, *, tm, tn, tk=None, out_dtype=BF16, epilogue="cast", extra=None, group=None, name="mm"):
    n_a = len(a_list)
    m, kc = a_list[0].shape
    k_total, n = w.shape
    assert kc * n_a == k_total
    tm = min(tm, m)
    tn = min(tn, n)
    tk = kc if tk is None else min(tk, kc)
    assert m % tm == 0 and n % tn == 0 and kc % tk == 0
    kt = kc // tk
    assert kt == 1 or n_a == 1
    kblocks_per_chunk = kc // tk

    in_specs = []
    for c in range(n_a):
        in_specs.append(pl.BlockSpec((tm, tk), lambda i, j, k: (i, k)))
    for c in range(n_a):
        in_specs.append(pl.BlockSpec((tk, tn), lambda i, j, k, c=c: (c * kblocks_per_chunk + k, j)))
    args = list(a_list) + [w] * n_a
    out_bytes = jnp.dtype(out_dtype).itemsize
    vmem = 2 * n_a * (tm * tk + tk * tn) * 2 + 2 * tm * tn * out_bytes + 2 * tm * tn * 4
    if epilogue == "residual":
        in_specs.append(pl.BlockSpec((tm, tn), lambda i, j, k: (i, j)))
        args.append(extra)
        vmem += 2 * tm * tn * 4
    elif epilogue == "headnorm":
        assert tn % group == 0
        in_specs.append(pl.BlockSpec((1, group), lambda i, j, k: (0, 0)))
        args.append(extra.reshape(1, group))
    scratch = [pltpu.VMEM((tm, tn), F32)] if kt > 1 else []
    if kt > 1:
        vmem += tm * tn * 4
    return pl.pallas_call(
        functools.partial(_mm_kernel, n_a=n_a, kt=kt, epilogue=epilogue, group=group),
        grid=(m // tm, n // tn, kt),
        in_specs=in_specs,
        out_specs=pl.BlockSpec((tm, tn), lambda i, j, k: (i, j)),
        out_shape=jax.ShapeDtypeStruct((m, n), out_dtype),
        scratch_shapes=scratch,
        compiler_params=pltpu.CompilerParams(
            dimension_semantics=("parallel", "parallel", "arbitrary"),
            vmem_limit_bytes=int(min(max(vmem + (6 << 20), 32 << 20), VMEM_LIMIT_CAP))),
        name=name,
    )(*args)


def _conv_kernel(ax_ref, ab_ref, ac_ref, buf_ref, w_ref, y_ref, new_ref, carry_ref, *, last_valid):
    t = pl.program_id(1)
    nt = pl.num_programs(1)
    rows = ax_ref.shape[0]
    u = ac_ref[...].astype(F32) * ax_ref[...].astype(F32)

    @pl.when(t == 0)
    def _():
        carry_ref[...] = buf_ref[0]

    prev = carry_ref[...]
    row = lax.broadcasted_iota(jnp.int32, u.shape, 0)
    u1 = jnp.where(row == 0, prev[1:2], pltpu.roll(u, 1, 0))
    u2 = jnp.where(row == 0, prev[0:1], jnp.where(row == 1, prev[1:2], pltpu.roll(u, 2, 0)))
    w = w_ref[...]
    conv = w[0:1] * u2 + w[1:2] * u1 + w[2:3] * u
    y_ref[...] = (ab_ref[...].astype(F32) * conv).astype(y_ref.dtype)
    carry_ref[...] = u[rows - 2:rows]

    @pl.when(t == nt - 1)
    def _():
        new_ref[0] = u[last_valid - 2:last_valid]


def short_conv(z, conv_buf, conv_w, *, batch, rows_per_seq, rows_blk, valid_rows, col_blocks):
    width = conv_w.shape[1]
    rows_blk = min(rows_blk, rows_per_seq)
    nt = rows_per_seq // rows_blk
    last_valid = valid_rows - (nt - 1) * rows_blk
    assert last_valid >= 2
    cx, cb, cc = col_blocks

    def zspec(cblk):
        return pl.BlockSpec((rows_blk, width), lambda b, t, cblk=cblk: (b * nt + t, cblk))

    return pl.pallas_call(
        functools.partial(_conv_kernel, last_valid=last_valid),
        grid=(batch, nt),
        in_specs=[zspec(cx), zspec(cb), zspec(cc),
                  pl.BlockSpec((1, 2, width), lambda b, t: (b, 0, 0)),
                  pl.BlockSpec((3, width), lambda b, t: (0, 0))],
        out_specs=[pl.BlockSpec((rows_blk, width), lambda b, t: (b * nt + t, 0)),
                   pl.BlockSpec((1, 2, width), lambda b, t: (b, 0, 0))],
        out_shape=[jax.ShapeDtypeStruct((batch * rows_per_seq, width), BF16),
                   jax.ShapeDtypeStruct((batch, 2, width), F32)],
        scratch_shapes=[pltpu.VMEM((2, width), F32)],
        compiler_params=_params(("parallel", "arbitrary"), rows_blk * width * 40),
        name="short_conv",
    )(z, z, z, conv_buf, conv_w)


def _cumsum_rows(x):
    rows = x.shape[0]
    row = lax.broadcasted_iota(jnp.int32, x.shape, 0)
    shift = 1
    while shift < rows:
        x = x + jnp.where(row >= shift, pltpu.roll(x, shift, 0), 0.0)
        shift *= 2
    return x


def _log_sigmoid(x):
    return jnp.minimum(x, 0.0) - jnp.log1p(jnp.exp(-jnp.abs(x)))


def _mlstm_kernel(q_ref, k_ref, v_ref, og_ref, g_ref, gbias_ref, gain_ref, c0_ref, n0_ref, m0_ref,
                  y_ref, c_ref, n_ref, m_ref, *, valid, heads, dk, dv):
    chunk = pl.program_id(1)
    L = CHUNK_ROWS
    lb = q_ref.shape[0]

    @pl.when(chunk == 0)
    def _():
        c_ref[...] = c0_ref[...]
        n_ref[...] = n0_ref[...]
        m_ref[...] = m0_ref[...]

    row = lax.broadcasted_iota(jnp.int32, (L, 1), 0)
    tri = lax.broadcasted_iota(jnp.int32, (L, L), 1) <= lax.broadcasted_iota(jnp.int32, (L, L), 0)
    row_ok = row < valid
    gates = _pad_rows(g_ref[...], L) + gbias_ref[...]
    ig_all = jnp.where(row_ok, gates, NEG)
    lf_all = jnp.where(row_ok, _log_sigmoid(gates), 0.0)
    b_all = _cumsum_rows(lf_all)
    ig_all_t = ig_all.T
    b_all_t = b_all.T
    scale = dk ** -0.5
    ys = []
    for h in range(heads):
        q = _pad_rows(q_ref[:, h * dk:(h + 1) * dk], L)
        k = _pad_rows(k_ref[:, h * dk:(h + 1) * dk], L)
        v = _pad_rows(v_ref[:, h * dv:(h + 1) * dv], L)
        ig_c = ig_all[:, h:h + 1]
        b_c = b_all[:, heads + h:heads + h + 1]
        ig_r = ig_all_t[h:h + 1, :]
        b_r = b_all_t[heads + h:heads + h + 1, :]
        c0 = c_ref[0, h]
        n0 = n_ref[0, h:h + 1, :]
        m0 = m_ref[0, :, h:h + 1]

        dmat = jnp.where(tri, b_c + (ig_r - b_r), NEG)
        inter = b_c + m0
        m_c = jnp.maximum(inter, jnp.max(dmat, axis=-1, keepdims=True))
        w_inter = jnp.exp(inter - m_c)
        qk = _nt_dot(q, k) * scale
        w_intra = jnp.exp(dmat - m_c) * qk
        num = w_inter * jnp.dot(q, c0.astype(BF16), preferred_element_type=F32)
        num = num + jnp.dot(w_intra.astype(BF16), v, preferred_element_type=F32)
        qn0 = jnp.sum(q.astype(F32) * n0, axis=-1, keepdims=True)
        den = w_inter * qn0 + jnp.sum(w_intra, axis=-1, keepdims=True)
        hid = num / jnp.maximum(jnp.abs(den), jnp.exp(-m_c))

        b_end = b_c[valid - 1:valid, :]
        m_end = m_c[valid - 1:valid, :]
        g_state = jnp.exp(b_end + m0 - m_end)
        g_tok = jnp.exp(b_end - b_c + ig_c - m_end) * scale
        kg = k.astype(F32) * g_tok
        c_ref[0, h] = g_state * c0 + jnp.dot(kg.T.astype(BF16), v, preferred_element_type=F32)
        n_ref[0, h:h + 1, :] = g_state * n0 + jnp.sum(kg, axis=0, keepdims=True)
        m_ref[0, :, h:h + 1] = m_end

        hn = _group_rms(hid, gain_ref[:, h * dv:(h + 1) * dv], dv)
        og = _pad_rows(og_ref[:, h * dv:(h + 1) * dv], L).astype(F32)
        ys.append(hn * jax.nn.sigmoid(og))
    y = jnp.concatenate(ys, axis=1)
    y_ref[...] = y[:lb].astype(y_ref.dtype)


def mlstm_mixer(z, gates, gate_bias, gain, c0, n0, m0, *, batch, rows_per_seq, valid_rows, cols):
    _, heads, dk, dv = c0.shape
    lb = min(CHUNK_ROWS, rows_per_seq)
    nc = rows_per_seq // lb
    assert rows_per_seq % lb == 0
    valid = valid_rows - (nc - 1) * lb
    assert (nc == 1 or valid == lb) and 1 <= valid <= lb
    q_off, k_off, v_off, o_off = cols
    wq, wv = heads * dk, heads * dv

    def zspec(width, off):
        assert off % width == 0
        return pl.BlockSpec((lb, width), lambda b, c, blk=off // width: (b * nc + c, blk))

    m0 = m0.reshape(batch, 1, heads)
    y, c, n, m = pl.pallas_call(
        functools.partial(_mlstm_kernel, valid=valid, heads=heads, dk=dk, dv=dv),
        grid=(batch, nc),
        in_specs=[zspec(wq, q_off), zspec(wq, k_off), zspec(wv, v_off), zspec(wv, o_off),
                  pl.BlockSpec((lb, V7X_LANES), lambda b, c: (b * nc + c, 0)),
                  pl.BlockSpec((1, V7X_LANES), lambda b, c: (0, 0)),
                  pl.BlockSpec((1, wv), lambda b, c: (0, 0)),
                  pl.BlockSpec((1, heads, dk, dv), lambda b, c: (b, 0, 0, 0)),
                  pl.BlockSpec((1, heads, dk), lambda b, c: (b, 0, 0)),
                  pl.BlockSpec((1, 1, heads), lambda b, c: (b, 0, 0))],
        out_specs=[pl.BlockSpec((lb, wv), lambda b, c: (b * nc + c, 0)),
                   pl.BlockSpec((1, heads, dk, dv), lambda b, c: (b, 0, 0, 0)),
                   pl.BlockSpec((1, heads, dk), lambda b, c: (b, 0, 0)),
                   pl.BlockSpec((1, 1, heads), lambda b, c: (b, 0, 0))],
        out_shape=[jax.ShapeDtypeStruct((batch * rows_per_seq, wv), BF16),
                   jax.ShapeDtypeStruct(c0.shape, F32),
                   jax.ShapeDtypeStruct(n0.shape, F32),
                   jax.ShapeDtypeStruct((batch, 1, heads), F32)],
        compiler_params=_params(("parallel", "arbitrary"), 16 << 20),
        name="mlstm",
    )(z, z, z, z, gates, gate_bias, gain.reshape(1, wv), c0, n0, m0)
    return y, c, n, m.reshape(batch, heads)


def _retention_kernel(q_ref, k_ref, v_ref, gd_ref, gain_ref, s0_ref, y_ref, s_ref, *, valid, heads, dk, dv):
    chunk = pl.program_id(1)
    L = CHUNK_ROWS
    lb = q_ref.shape[0]

    @pl.when(chunk == 0)
    def _():
        s_ref[...] = s0_ref[...]

    row = lax.broadcasted_iota(jnp.int32, (L, 1), 0)
    t_idx = lax.broadcasted_iota(jnp.int32, (L, L), 0)
    s_idx = lax.broadcasted_iota(jnp.int32, (L, L), 1)
    dist = (t_idx - s_idx).astype(F32)
    keep = (t_idx >= s_idx) & (s_idx < valid)
    rowf = row.astype(F32)
    scale = dk ** -0.5
    ys = []
    for h in range(heads):
        log_gamma = math.log1p(-(2.0 ** (-5.0 - h)))
        q = _pad_rows(q_ref[:, h * dk:(h + 1) * dk], L)
        k = _pad_rows(k_ref[:, h * dk:(h + 1) * dk], L)
        v = _pad_rows(v_ref[:, h * dv:(h + 1) * dv], L)
        s0 = s_ref[0, h]
        decay = jnp.where(keep, jnp.exp(jnp.maximum(dist, 0.0) * log_gamma), 0.0)
        smat = _nt_dot(q, k) * scale * decay
        inter = jnp.exp((rowf + 1.0) * log_gamma)
        o = jnp.dot(smat.astype(BF16), v, preferred_element_type=F32)
        o = o + inter * jnp.dot(q, s0.astype(BF16), preferred_element_type=F32)
        tail = jnp.where(row < valid, jnp.exp((valid - 1.0 - rowf) * log_gamma), 0.0) * scale
        kt = k.astype(F32) * tail
        s_ref[0, h] = math.exp(valid * log_gamma) * s0 + jnp.dot(kt.T.astype(BF16), v, preferred_element_type=F32)
        on = _group_rms(o, gain_ref[:, h * dv:(h + 1) * dv], dv)
        gd = _pad_rows(gd_ref[:, h * dv:(h + 1) * dv], L).astype(F32)
        ys.append(on * (gd * jax.nn.sigmoid(gd)))
    y = jnp.concatenate(ys, axis=1)
    y_ref[...] = y[:lb].astype(y_ref.dtype)


def retention_mixer(z, gain, s0, *, batch, rows_per_seq, valid_rows, cols):
    _, heads, dk, dv = s0.shape
    lb = min(CHUNK_ROWS, rows_per_seq)
    nc = rows_per_seq // lb
    valid = valid_rows - (nc - 1) * lb
    assert (nc == 1 or valid == lb) and 1 <= valid <= lb
    q_off, k_off, v_off, g_off = cols
    wq, wv = heads * dk, heads * dv

    def zspec(width, off):
        assert off % width == 0
        return pl.BlockSpec((lb, width), lambda b, c, blk=off // width: (b * nc + c, blk))

    return pl.pallas_call(
        functools.partial(_retention_kernel, valid=valid, heads=heads, dk=dk, dv=dv),
        grid=(batch, nc),
        in_specs=[zspec(wq, q_off), zspec(wq, k_off), zspec(wv, v_off), zspec(wv, g_off),
                  pl.BlockSpec((1, wv), lambda b, c: (0, 0)),
                  pl.BlockSpec((1, heads, dk, dv), lambda b, c: (b, 0, 0, 0))],
        out_specs=[pl.BlockSpec((lb, wv), lambda b, c: (b * nc + c, 0)),
                   pl.BlockSpec((1, heads, dk, dv), lambda b, c: (b, 0, 0, 0))],
        out_shape=[jax.ShapeDtypeStruct((batch * rows_per_seq, wv), BF16),
                   jax.ShapeDtypeStruct(s0.shape, F32)],
        compiler_params=_params(("parallel", "arbitrary"), 16 << 20),
        name="retention",
    )(z, z, z, z, gain.reshape(1, wv), s0)


def _diff_prep_kernel(q_ref, k_ref, v_ref, gq_ref, gk_ref, qn_ref, kn_ref, vc_ref, *, dh):
    qn_ref[...] = _group_rms(q_ref[...].astype(F32), gq_ref[...], dh).astype(qn_ref.dtype)
    kn_ref[...] = _group_rms(k_ref[...].astype(F32), gk_ref[...], dh)
    vc_ref[...] = v_ref[...].astype(F32)


def diff_prep(z, g_q, g_k, *, rows_blk, cols, width):
    m = z.shape[0]
    dh = g_q.shape[0]
    rows_blk = min(rows_blk, m)
    q_off, k_off, v_off = cols

    def zspec(off):
        assert off % width == 0
        return pl.BlockSpec((rows_blk, width), lambda i, blk=off // width: (i, blk))

    ospec = pl.BlockSpec((rows_blk, width), lambda i: (i, 0))
    return pl.pallas_call(
        functools.partial(_diff_prep_kernel, dh=dh),
        grid=(m // rows_blk,),
        in_specs=[zspec(q_off), zspec(k_off), zspec(v_off),
                  pl.BlockSpec((1, dh), lambda i: (0, 0)), pl.BlockSpec((1, dh), lambda i: (0, 0))],
        out_specs=[ospec, ospec, ospec],
        out_shape=[jax.ShapeDtypeStruct((m, width), BF16),
                   jax.ShapeDtypeStruct((m, width), F32),
                   jax.ShapeDtypeStruct((m, width), F32)],
        compiler_params=_params(("parallel",), rows_blk * width * 48),
        name="diff_prep",
    )(z, z, z, g_q.reshape(1, dh), g_k.reshape(1, dh))


def _softmax_update(s, v, m_ref, l_ref, acc_ref):
    m_old = m_ref[...]
    m_new = jnp.maximum(m_old, jnp.max(s, axis=-1, keepdims=True))
    alpha = jnp.exp(m_old - m_new)
    p = jnp.exp(s - m_new)
    l_ref[...] = alpha * l_ref[...] + jnp.sum(p, axis=-1, keepdims=True)
    acc_ref[...] = alpha * acc_ref[...] + jnp.dot(p.astype(BF16), v, preferred_element_type=F32)
    m_ref[...] = m_new


def _diff_attn_prompt_kernel(slope_ref, lam_ref, q_ref, k_ref, v_ref, gain_ref, y_ref, m_s, l_s, acc_s,
                             *, bq, bk, dh, out_scale):
    h = pl.program_id(1)
    qi = pl.program_id(2)
    ki = pl.program_id(3)

    @pl.when(ki == 0)
    def _():
        m_s[...] = jnp.full(m_s.shape, NEG, F32)
        l_s[...] = jnp.zeros(l_s.shape, F32)
        acc_s[...] = jnp.zeros(acc_s.shape, F32)

    @pl.when(ki <= qi)
    def _():
        q = q_ref[...]
        k = k_ref[...].astype(BF16)
        v = v_ref[...]
        rel = lax.broadcasted_iota(jnp.int32, (1, bk), 1) + (ki * bk - qi * bq)
        bias = slope_ref[h] * rel.astype(F32)
        qrow = lax.broadcasted_iota(jnp.int32, (bq, 1), 0)
        masked_bias = jnp.where(rel <= qrow, bias, NEG)
        for j in range(2):
            s = _nt_dot(q[:, j * dh:(j + 1) * dh], k[:, j * dh:(j + 1) * dh]) * dh ** -0.5 + masked_bias
            _softmax_update(s, v, m_s.at[j], l_s.at[j], acc_s.at[j])

    @pl.when(ki == qi)
    def _():
        o = acc_s[0] / l_s[0] - lam_ref[0] * (acc_s[1] / l_s[1])
        y_ref[...] = (_group_rms(o, gain_ref[...], o.shape[1]) * out_scale).astype(y_ref.dtype)


def diff_attention_prompt(qn, kn, z, slopes, lam, gain, *, batch, seq, heads, dh, dv, v_off, blk, out_scale):
    bq = bk = min(blk, seq)
    nq = seq // bq
    hw = 2 * dh
    assert hw == dv and v_off % dv == 0
    vblk = v_off // dv

    def kv_row(b, qi, ki):
        return b * nq + jnp.minimum(ki, qi)

    grid_spec = pltpu.PrefetchScalarGridSpec(
        num_scalar_prefetch=0,
        grid=(batch, heads, nq, nq),
        in_specs=[pl.BlockSpec(memory_space=pltpu.SMEM),
                  pl.BlockSpec(memory_space=pltpu.SMEM),
                  pl.BlockSpec((bq, hw), lambda b, h, qi, ki: (b * nq + qi, h)),
                  pl.BlockSpec((bk, hw), lambda b, h, qi, ki: (kv_row(b, qi, ki), h)),
                  pl.BlockSpec((bk, dv), lambda b, h, qi, ki: (kv_row(b, qi, ki), vblk + h)),
                  pl.BlockSpec((1, dv), lambda b, h, qi, ki: (0, h))],
        out_specs=pl.BlockSpec((bq, dv), lambda b, h, qi, ki: (b * nq + qi, h)),
        scratch_shapes=[pltpu.VMEM((2, bq, 1), F32), pltpu.VMEM((2, bq, 1), F32), pltpu.VMEM((2, bq, dv), F32)],
    )
    return pl.pallas_call(
        functools.partial(_diff_attn_prompt_kernel, bq=bq, bk=bk, dh=dh, out_scale=out_scale),
        grid_spec=grid_spec,
        out_shape=jax.ShapeDtypeStruct((batch * seq, heads * dv), BF16),
        compiler_params=_params(("parallel", "parallel", "parallel", "arbitrary"), 24 << 20),
        name="diff_attn_prompt",
    )(slopes, lam, qn, kn, z, gain.reshape(1, heads * dv))


def _diff_attn_decode_kernel(pt_ref, slope_ref, lam_ref, q_ref, kn_ref, vn_ref, gain_ref, *rest,
                             pages, page, past, valid, heads, dh, dv, out_scale):
    k_refs = rest[:pages]
    v_refs = rest[pages:2 * pages]
    y_ref, m_s, l_s, acc_s = rest[2 * pages:]
    step = pl.program_id(1)
    nsteps = pl.num_programs(1)
    rows = q_ref.shape[0]
    hw = 2 * dh

    @pl.when(step == 0)
    def _():
        m_s[...] = jnp.full(m_s.shape, NEG, F32)
        l_s[...] = jnp.zeros(l_s.shape, F32)
        acc_s[...] = jnp.zeros(acc_s.shape, F32)

    def stacked_queries(h):
        qh = q_ref[:, h * hw:(h + 1) * hw]
        zero = jnp.zeros((rows, dh), qh.dtype)
        return jnp.concatenate([jnp.concatenate([qh[:, :dh], zero], axis=1),
                                jnp.concatenate([zero, qh[:, dh:]], axis=1)], axis=0)

    nkeys = pages * page
    rel = lax.broadcasted_iota(jnp.int32, (1, nkeys), 1) + (step * nkeys - past)
    for h in range(heads):
        qs = stacked_queries(h)
        k = jnp.concatenate([r[0, :, h * hw:(h + 1) * hw] for r in k_refs], axis=0).astype(BF16)
        v = jnp.concatenate([r[0, :, h * dv:(h + 1) * dv] for r in v_refs], axis=0).astype(BF16)
        s = _nt_dot(qs, k) * dh ** -0.5 + slope_ref[h] * rel.astype(F32)
        _softmax_update(s, v, m_s.at[h], l_s.at[h], acc_s.at[h])

    @pl.when(step == nsteps - 1)
    def _():
        nk = V7X_LANES
        key = lax.broadcasted_iota(jnp.int32, (1, nk), 1)
        tok = lax.broadcasted_iota(jnp.int32, (2 * rows, 1), 0) & (rows - 1)
        ok = (key <= tok) & (key < valid)
        for h in range(heads):
            qs = stacked_queries(h)
            k = _pad_rows(kn_ref[:, h * hw:(h + 1) * hw], nk).astype(BF16)
            v = _pad_rows(vn_ref[:, h * dv:(h + 1) * dv], nk)
            s = _nt_dot(qs, k) * dh ** -0.5 + slope_ref[h] * key.astype(F32)
            s = jnp.where(ok, s, NEG)
            _softmax_update(s, v, m_s.at[h], l_s.at[h], acc_s.at[h])
            o = acc_s[h] / l_s[h]
            o = o[:rows] - lam_ref[0] * o[rows:]
            y_ref[:, h * dv:(h + 1) * dv] = (
                _group_rms(o, gain_ref[:, h * dv:(h + 1) * dv], dv) * out_scale).astype(y_ref.dtype)


def diff_attention_decode(qn, kn, z, cache_k, cache_v, page_table, layer, slopes, lam, gain,
                          *, batch, rows_per_seq, valid_rows, heads, dh, dv, v_off, out_scale):
    depth, n_pool, page = cache_k.shape[:3]
    n_pages = page_table.shape[1]
    pages = PAGES_PER_STEP
    assert n_pages % pages == 0
    nsteps = n_pages // pages
    width = heads * dv
    ck = cache_k.reshape(depth * n_pool, page, width)
    cv = cache_v.reshape(depth * n_pool, page, width)
    base = layer * n_pool

    def page_spec(p):
        return pl.BlockSpec((1, page, width),
                            lambda b, s, pt, p=p: (base + pt[b * n_pages + s * pages + p], 0, 0))

    row_spec = pl.BlockSpec((rows_per_seq, width), lambda b, s, pt: (b, 0))
    grid_spec = pltpu.PrefetchScalarGridSpec(
        num_scalar_prefetch=1,
        grid=(batch, nsteps),
        in_specs=[pl.BlockSpec(memory_space=pltpu.SMEM),
                  pl.BlockSpec(memory_space=pltpu.SMEM),
                  row_spec, row_spec,
                  pl.BlockSpec((rows_per_seq, width), lambda b, s, pt: (b, v_off // width)),
                  pl.BlockSpec((1, width), lambda b, s, pt: (0, 0))]
                 + [page_spec(p) for p in range(pages)] + [page_spec(p) for p in range(pages)],
        out_specs=row_spec,
        scratch_shapes=[pltpu.VMEM((heads, 2 * rows_per_seq, 1), F32),
                        pltpu.VMEM((heads, 2 * rows_per_seq, 1), F32),
                        pltpu.VMEM((heads, 2 * rows_per_seq, dv), F32)],
    )
    assert v_off % width == 0
    return pl.pallas_call(
        functools.partial(_diff_attn_decode_kernel, pages=pages, page=page, past=n_pages * page,
                          valid=valid_rows, heads=heads, dh=dh, dv=dv, out_scale=out_scale),
        grid_spec=grid_spec,
        out_shape=jax.ShapeDtypeStruct((batch * rows_per_seq, width), BF16),
        compiler_params=_params(("parallel", "arbitrary"), 4 * pages * page * width * 4 + (8 << 20)),
        name="diff_attn_decode",
    )(page_table.reshape(-1), slopes, lam, qn, kn, z, gain.reshape(1, width),
      *([ck] * pages), *([cv] * pages))


def _cross_attn_kernel(q_ref, mk_ref, mv_ref, o_ref, *, heads, dh):
    outs = []
    for h in range(heads):
        q = q_ref[:, h * dh:(h + 1) * dh]
        mk = mk_ref[:, h * dh:(h + 1) * dh].astype(BF16)
        mv = mv_ref[:, h * dh:(h + 1) * dh].astype(BF16)
        s = _nt_dot(q, mk) * dh ** -0.5
        p = jnp.exp(s - jnp.max(s, axis=-1, keepdims=True))
        o = jnp.dot(p.astype(BF16), mv, preferred_element_type=F32)
        outs.append(o / jnp.sum(p, axis=-1, keepdims=True))
    o_ref[...] = jnp.concatenate(outs, axis=1).astype(o_ref.dtype)


def cross_attention(qx, mk, mv, *, batch, rows_per_seq, n_mem, heads, dh, rows_blk):
    width = heads * dh
    rows_blk = min(rows_blk, rows_per_seq)
    nq = rows_per_seq // rows_blk
    return pl.pallas_call(
        functools.partial(_cross_attn_kernel, heads=heads, dh=dh),
        grid=(batch, nq),
        in_specs=[pl.BlockSpec((rows_blk, width), lambda b, i: (b * nq + i, 0)),
                  pl.BlockSpec((n_mem, width), lambda b, i: (b, 0)),
                  pl.BlockSpec((n_mem, width), lambda b, i: (b, 0))],
        out_specs=pl.BlockSpec((rows_blk, width), lambda b, i: (b * nq + i, 0)),
        out_shape=jax.ShapeDtypeStruct((batch * rows_per_seq, width), BF16),
        compiler_params=_params(("parallel", "parallel"), 24 << 20),
        name="cross_attn",
    )(qx, mk, mv)


def _split_w_in(w_in_l, sizes, heads):
    offs = [0]
    for s in sizes:
        offs.append(offs[-1] + s)
    g0, g1 = offs[7], offs[9]
    main = jnp.concatenate([w_in_l[:, :g0], w_in_l[:, g1:]], axis=1).astype(BF16)
    gate = jnp.pad(w_in_l[:, g0:g1], ((0, 0), (0, V7X_LANES - 2 * heads))).astype(BF16)
    return main, gate


def kernel(x_prompt, x_sample, cache_diff_k, cache_diff_v, cache_mem_k, cache_mem_v, state_conv, state_mlstm_C, state_mlstm_n, state_mlstm_m, state_ret, page_table, mem_prompt, norm_mix, w_in, conv_w, b_igate, b_fgate, g_mlstm, g_diff_q, g_diff_k, lam_q1, lam_k1, lam_q2, lam_k2, g_diff_out, g_ret, w_out, norm_xattn, norm_mem, w_xq, w_xk, w_xv, g_xq, g_xk, w_xo, norm_mlp, w_up, w_down):
    depth = w_in.shape[0]
    bp, seq, d_model = x_prompt.shape
    bs, dec = x_sample.shape[:2]
    conv_ch = conv_w.shape[2]
    _, _, hb, dkb, dvb = state_mlstm_C.shape
    _, _, hd, dkd, dvd = state_ret.shape
    hc, dvc = cache_diff_v.shape[3:]
    dhc = dvc // 2
    _, _, n_mem, hx, dhx = cache_mem_k.shape
    xw = hx * dhx
    assert seq % CHUNK_ROWS == 0 and dec <= SAMPLE_ROWS

    sizes = (conv_ch, conv_ch, conv_ch, hb * dkb, hb * dkb, hb * dvb, hb * dvb, hb, hb,
             2 * hc * dhc, 2 * hc * dhc, hc * dvc, hd * dkd, hd * dkd, hd * dvd, hd * dvd)
    names = ("a_x", "a_b", "a_c", "q_b", "k_b", "v_b", "o_b", "i_b", "f_b",
             "q_c", "k_c", "v_c", "q_d", "k_d", "v_d", "g_d")
    off, pos = {}, 0
    for nm, sz in zip(names, sizes):
        if nm not in ("i_b", "f_b"):
            off[nm] = pos
            pos += sz

    slopes = (2.0 ** (-8.0 * jnp.arange(1, hc + 1, dtype=F32) / hc)).astype(F32)

    x_p = x_prompt.reshape(bp * seq, d_model)
    x_s = jnp.pad(x_sample, ((0, 0), (0, SAMPLE_ROWS - dec), (0, 0))).reshape(bs * SAMPLE_ROWS, d_model)
    mem_rows = mem_prompt.reshape(bp * n_mem, d_model)

    groups = (
        dict(batch=bp, rows=seq, valid=seq, tm=1024, prompt=True),
        dict(batch=bs, rows=SAMPLE_ROWS, valid=dec, tm=bs * SAMPLE_ROWS, prompt=False),
    )
    xs = [x_p, x_s]
    outs = [dict(conv=[], C=[], n=[], m=[], k=[], v=[], S=[], mk=[], mv=[]) for _ in groups]

    for li in range(depth):
        lam_init = 0.8 - 0.6 * math.exp(-0.3 * li)
        lam = (jnp.exp(jnp.sum(lam_q1[li] * lam_k1[li])) - jnp.exp(jnp.sum(lam_q2[li] * lam_k2[li]))
               + lam_init).reshape(1).astype(F32)
        w_main, w_gate = _split_w_in(w_in[li], sizes, hb)
        w_out_b = w_out[li].astype(BF16)
        w_xq_b = w_xq[li].astype(BF16)
        w_xo_b = w_xo[li].astype(BF16)
        w_up_b = w_up[li].astype(BF16)
        w_down_b = w_down[li].astype(BF16)
        gate_bias = jnp.pad(jnp.concatenate([b_igate[li], b_fgate[li]]), (0, V7X_LANES - 2 * hb)).reshape(1, V7X_LANES)

        hm = rmsnorm_rows(mem_rows, norm_mem[li], 256)
        mk_p = matmul([hm], w_xk[li].astype(BF16), tm=1024, tn=1024, out_dtype=F32,
                      epilogue="headnorm", extra=g_xk[li], group=dhx, name="mm_mem_k")
        mv_p = matmul([hm], w_xv[li].astype(BF16), tm=1024, tn=1024, out_dtype=F32, name="mm_mem_v")

        for gi, g in enumerate(groups):
            x = xs[gi]
            batch, rows, valid, tm = g["batch"], g["rows"], g["valid"], g["tm"]
            if g["prompt"]:
                conv_buf = jnp.zeros((batch, 2, conv_ch), F32)
                c0 = jnp.zeros((batch, hb, dkb, dvb), F32)
                n0 = jnp.zeros((batch, hb, dkb), F32)
                m0 = jnp.zeros((batch, hb), F32)
                s0 = jnp.zeros((batch, hd, dkd, dvd), F32)
                mk, mv = mk_p, mv_p
            else:
                conv_buf, c0, n0, m0, s0 = (state_conv[li], state_mlstm_C[li], state_mlstm_n[li],
                                            state_mlstm_m[li], state_ret[li])
                mk = cache_mem_k[li].reshape(batch * n_mem, xw)
                mv = cache_mem_v[li].reshape(batch * n_mem, xw)

            h = rmsnorm_rows(x, norm_mix[li], 256)
            z = matmul([h], w_main, tm=tm, tn=1024, name="mm_in")
            gates = matmul([h], w_gate, tm=tm, tn=V7X_LANES, out_dtype=F32, name="mm_gate")
            y_a, conv_new = short_conv(z, conv_buf, conv_w[li], batch=batch, rows_per_seq=rows, rows_blk=512,
                                       valid_rows=valid,
                                       col_blocks=(off["a_x"] // conv_ch, off["a_b"] // conv_ch, off["a_c"] // conv_ch))
            y_b, c_new, n_new, m_new = mlstm_mixer(
                z, gates, gate_bias, g_mlstm[li], c0, n0, m0, batch=batch, rows_per_seq=rows, valid_rows=valid,
                cols=(off["q_b"], off["k_b"], off["v_b"], off["o_b"]))
            qn, kn, vc = diff_prep(z, g_diff_q[li], g_diff_k[li], rows_blk=512,
                                   cols=(off["q_c"], off["k_c"], off["v_c"]), width=hc * dvc)
            if g["prompt"]:
                y_c = diff_attention_prompt(qn, kn, z, slopes, lam, g_diff_out[li], batch=batch, seq=rows,
                                            heads=hc, dh=dhc, dv=dvc, v_off=off["v_c"], blk=512,
                                            out_scale=1.0 - lam_init)
            else:
                y_c = diff_attention_decode(qn, kn, z, cache_diff_k, cache_diff_v, page_table, li, slopes, lam,
                                            g_diff_out[li], batch=batch, rows_per_seq=rows, valid_rows=valid,
                                            heads=hc, dh=dhc, dv=dvc, v_off=off["v_c"], out_scale=1.0 - lam_init)
            y_d, s_new = retention_mixer(z, g_ret[li], s0, batch=batch, rows_per_seq=rows, valid_rows=valid,
                                         cols=(off["q_d"], off["k_d"], off["v_d"], off["g_d"]))
            x = matmul([y_a, y_b, y_c, y_d], w_out_b, tm=tm, tn=512, out_dtype=F32,
                       epilogue="residual", extra=x, name="mm_out")

            h2 = rmsnorm_rows(x, norm_xattn[li], 256)
            qx = matmul([h2], w_xq_b, tm=tm, tn=1024, epilogue="headnorm", extra=g_xq[li], group=dhx, name="mm_xq")
            ox = cross_attention(qx, mk, mv, batch=batch, rows_per_seq=rows, n_mem=n_mem, heads=hx, dh=dhx,
                                 rows_blk=512)
            x = matmul([ox], w_xo_b, tm=tm, tn=1024, out_dtype=F32, epilogue="residual", extra=x, name="mm_xo")

            h3 = rmsnorm_rows(x, norm_mlp[li], 256)
            hid = matmul([h3], w_up_b, tm=tm, tn=1024, epilogue="relu2", name="mm_up")
            x = matmul([hid], w_down_b, tm=tm, tn=512, tk=4096, out_dtype=F32,
                       epilogue="residual", extra=x, name="mm_down")

            xs[gi] = x
            o = outs[gi]
            o["conv"].append(conv_new); o["C"].append(c_new); o["n"].append(n_new); o["m"].append(m_new)
            o["k"].append(kn); o["v"].append(vc); o["S"].append(s_new)
            if g["prompt"]:
                o["mk"].append(mk_p); o["mv"].append(mv_p)

    op, os_ = outs
    y_prompt = xs[0].reshape(bp, seq, d_model)
    y_sample = xs[1].reshape(bs, SAMPLE_ROWS, d_model)[:, :dec]

    def cache_rows(parts, batch, rows, valid):
        a = jnp.stack(parts).reshape(depth, batch, rows, hc, dvc)
        return a if rows == valid else a[:, :, :valid]

    return (y_prompt, y_sample,
            jnp.stack(op["conv"]), jnp.stack(op["C"]), jnp.stack(op["n"]), jnp.stack(op["m"]),
            cache_rows(op["k"], bp, seq, seq), cache_rows(op["v"], bp, seq, seq), jnp.stack(op["S"]),
            jnp.stack(op["mk"]).reshape(depth, bp, n_mem, hx, dhx),
            jnp.stack(op["mv"]).reshape(depth, bp, n_mem, hx, dhx),
            jnp.stack(os_["conv"]), jnp.stack(os_["C"]), jnp.stack(os_["n"]), jnp.stack(os_["m"]),
            cache_rows(os_["k"], bs, SAMPLE_ROWS, dec), cache_rows(os_["v"], bs, SAMPLE_ROWS, dec),
            jnp.stack(os_["S"]))
```

```python
import functools
import math

import jax
import jax.numpy as jnp
from jax import lax
from jax.experimental import pallas as pl
from jax.experimental.pallas import tpu as pltpu

V7X_LANES = 128
V7X_VMEM_BYTES = 64 * 1024 * 1024
VMEM_LIMIT_CAP = V7X_VMEM_BYTES - 6 * 1024 * 1024

EPS = 1e-6
NEG = -1e30
CHUNK_ROWS = 128
SAMPLE_ROWS = 16
PAGES_PER_STEP = 4

BF16 = jnp.bfloat16
F32 = jnp.float32


def _params(semantics, vmem_estimate):
    limit = int(min(max(vmem_estimate * 5 // 4 + (4 << 20), 32 << 20), VMEM_LIMIT_CAP))
    return pltpu.CompilerParams(dimension_semantics=semantics, vmem_limit_bytes=limit)


def _nt_dot(a, b):
    return lax.dot_general(a, b, (((1,), (1,)), ((), ())), preferred_element_type=F32)


def _pad_rows(x, rows):
    if x.shape[0] == rows:
        return x
    return jnp.concatenate([x, jnp.zeros((rows - x.shape[0],) + x.shape[1:], x.dtype)], axis=0)


def _group_rms(x, gain, width):
    parts = []
    for g in range(x.shape[1] // width):
        blk = x[:, g * width:(g + 1) * width]
        ms = jnp.mean(blk * blk, axis=-1, keepdims=True)
        parts.append(blk * lax.rsqrt(ms + EPS) * gain)
    return parts[0] if len(parts) == 1 else jnp.concatenate(parts, axis=1)


def _rmsnorm_kernel(x_ref, g_ref, o_ref):
    x = x_ref[...]
    ms = jnp.mean(x * x, axis=-1, keepdims=True)
    o_ref[...] = (x * lax.rsqrt(ms + EPS) * g_ref[...]).astype(o_ref.dtype)


def rmsnorm_rows(x, g, rows_blk):
    m, d = x.shape
    rows_blk = min(rows_blk, m)
    return pl.pallas_call(
        _rmsnorm_kernel,
        grid=(m // rows_blk,),
        in_specs=[pl.BlockSpec((rows_blk, d), lambda i: (i, 0)),
                  pl.BlockSpec((1, d), lambda i: (0, 0))],
        out_specs=pl.BlockSpec((rows_blk, d), lambda i: (i, 0)),
        out_shape=jax.ShapeDtypeStruct((m, d), BF16),
        compiler_params=_params(("parallel",), rows_blk * d * 12),
        name="rmsnorm",
    )(x, g.reshape(1, d))


def _cast_kernel(x_ref, o_ref):
    o_ref[...] = x_ref[...].astype(o_ref.dtype)


def cast_bf16(w, rows_blk=256, cols_blk=4096):
    d, k, n = w.shape
    tr, tc = min(rows_blk, k), min(cols_blk, n)
    assert k % tr == 0 and n % tc == 0
    spec = pl.BlockSpec((None, tr, tc), lambda l, i, j: (l, i, j))
    return pl.pallas_call(
        _cast_kernel,
        grid=(d, k // tr, n // tc),
        in_specs=[spec], out_specs=spec,
        out_shape=jax.ShapeDtypeStruct(w.shape, BF16),
        compiler_params=_params(("parallel", "parallel", "parallel"), tr * tc * 16),
        name="cast_bf16",
    )(w)


def _transpose_cast_kernel(x_ref, o_ref):
    o_ref[...] = x_ref[0].T.astype(o_ref.dtype)


def split_w_in(w_in, g0, g1, tn=1024, tk=512):
    d, k, n = w_in.shape
    glen = g1 - g0
    n_main = n - glen
    assert g0 % tn == 0 and n_main % tn == 0 and k % tk == 0 and glen % 8 == 0 and glen <= V7X_LANES
    w_t = jnp.swapaxes(w_in, 1, 2)

    def src_index(l, j, kk):
        row = j * tn + jnp.where(j * tn >= g0, glen, 0)
        return (l, pl.multiple_of(row, 8), pl.multiple_of(kk * tk, V7X_LANES))

    main = pl.pallas_call(
        _transpose_cast_kernel,
        grid=(d, n_main // tn, k // tk),
        in_specs=[pl.BlockSpec((pl.Element(1), pl.Element(tn), pl.Element(tk)), src_index)],
        out_specs=pl.BlockSpec((None, tk, tn), lambda l, j, kk: (l, kk, j)),
        out_shape=jax.ShapeDtypeStruct((d, k, n_main), BF16),
        compiler_params=_params(("parallel", "parallel", "parallel"), tn * tk * 24),
        name="split_w_in",
    )(w_t)
    gate = jnp.pad(w_in[:, :, g0:g1], ((0, 0), (0, 0), (0, V7X_LANES - glen))).astype(BF16)
    return main, gate


def _mm_kernel(*refs, n_a, kt, epilogue, group):
    a_refs = refs[:n_a]
    w_refs = refs[n_a:2 * n_a]
    pos = 2 * n_a
    extra_ref = None
    if epilogue in ("residual", "headnorm"):
        extra_ref = refs[pos]
        pos += 1
    o_ref = refs[pos]
    acc_ref = refs[pos + 1] if kt > 1 else None

    part = jnp.dot(a_refs[0][...], w_refs[0][...], preferred_element_type=F32)
    for a_ref, w_ref in zip(a_refs[1:], w_refs[1:]):
        part = part + jnp.dot(a_ref[...], w_ref[...], preferred_element_type=F32)

    def finish(acc):
        if epilogue == "residual":
            o_ref[...] = extra_ref[...] + acc
        elif epilogue == "relu2":
            o_ref[...] = jnp.square(jnp.maximum(acc, 0.0)).astype(o_ref.dtype)
        elif epilogue == "headnorm":
            o_ref[...] = _group_rms(acc, extra_ref[...], group).astype(o_ref.dtype)
        else:
            o_ref[...] = acc.astype(o_ref.dtype)

    if kt == 1:
        finish(part)
    else:
        k = pl.program_id(2)

        @pl.when(k == 0)
        def _():
            acc_ref[...] = part

        @pl.when(k > 0)
        def _():
            acc_ref[...] += part

        @pl.when(k == kt - 1)
        def _():
            finish(acc_ref[...])


def matmul(a_list, w, layer, *, tm, tn, tk=None, out_dtype=BF16, epilogue="cast", extra=None, group=None, name="mm"):
    n_a = len(a_list)
    m, kc = a_list[0].shape
    _, k_total, n = w.shape
    assert kc * n_a == k_total
    tm = min(tm, m)
    tn = min(tn, n)
    tk = kc if tk is None else min(tk, kc)
    assert m % tm == 0 and n % tn == 0 and kc % tk == 0
    kt = kc // tk
    assert kt == 1 or n_a == 1
    kblocks_per_chunk = kc // tk

    in_specs = []
    for c in range(n_a):
        in_specs.append(pl.BlockSpec((tm, tk), lambda i, j, k: (i, k)))
    for c in range(n_a):
        in_specs.append(pl.BlockSpec((None, tk, tn), lambda i, j, k, c=c: (layer, c * kblocks_per_chunk + k, j)))
    args = list(a_list) + [w] * n_a
    out_bytes = jnp.dtype(out_dtype).itemsize
    vmem = 2 * n_a * (tm * tk + tk * tn) * 2 + 2 * tm * tn * out_bytes + 2 * tm * tn * 4
    if epilogue == "residual":
        in_specs.append(pl.BlockSpec((tm, tn), lambda i, j, k: (i, j)))
        args.append(extra)
        vmem += 2 * tm * tn * 4
    elif epilogue == "headnorm":
        assert tn % group == 0
        in_specs.append(pl.BlockSpec((1, group), lambda i, j, k: (0, 0)))
        args.append(extra.reshape(1, group))
    scratch = [pltpu.VMEM((tm, tn), F32)] if kt > 1 else []
    if kt > 1:
        vmem += tm * tn * 4
    return pl.pallas_call(
        functools.partial(_mm_kernel, n_a=n_a, kt=kt, epilogue=epilogue, group=group),
        grid=(m // tm, n // tn, kt),
        in_specs=in_specs,
        out_specs=pl.BlockSpec((tm, tn), lambda i, j, k: (i, j)),
        out_shape=jax.ShapeDtypeStruct((m, n), out_dtype),
        scratch_shapes=scratch,
        compiler_params=pltpu.CompilerParams(
            dimension_semantics=("parallel", "parallel", "arbitrary"),
            vmem_limit_bytes=int(min(max(vmem + (6 << 20), 32 << 20), VMEM_LIMIT_CAP))),
        name=name,
    )(*args)


def _conv_kernel(ax_ref, ab_ref, ac_ref, buf_ref, w_ref, y_ref, new_ref, carry_ref, *, last_valid):
    t = pl.program_id(1)
    nt = pl.num_programs(1)
    rows = ax_ref.shape[0]
    u = ac_ref[...].astype(F32) * ax_ref[...].astype(F32)

    @pl.when(t == 0)
    def _():
        carry_ref[...] = buf_ref[0]

    prev = carry_ref[...]
    row = lax.broadcasted_iota(jnp.int32, u.shape, 0)
    u1 = jnp.where(row == 0, prev[1:2], pltpu.roll(u, 1, 0))
    u2 = jnp.where(row == 0, prev[0:1], jnp.where(row == 1, prev[1:2], pltpu.roll(u, 2, 0)))
    w = w_ref[...]
    conv = w[0:1] * u2 + w[1:2] * u1 + w[2:3] * u
    y_ref[...] = (ab_ref[...].astype(F32) * conv).astype(y_ref.dtype)
    carry_ref[...] = u[rows - 2:rows]

    @pl.when(t == nt - 1)
    def _():
        new_ref[0] = u[last_valid - 2:last_valid]


def short_conv(z, conv_buf, conv_w, *, batch, rows_per_seq, rows_blk, valid_rows, col_blocks):
    width = conv_w.shape[1]
    rows_blk = min(rows_blk, rows_per_seq)
    nt = rows_per_seq // rows_blk
    last_valid = valid_rows - (nt - 1) * rows_blk
    assert last_valid >= 2
    cx, cb, cc = col_blocks

    def zspec(cblk):
        return pl.BlockSpec((rows_blk, width), lambda b, t, cblk=cblk: (b * nt + t, cblk))

    return pl.pallas_call(
        functools.partial(_conv_kernel, last_valid=last_valid),
        grid=(batch, nt),
        in_specs=[zspec(cx), zspec(cb), zspec(cc),
                  pl.BlockSpec((1, 2, width), lambda b, t: (b, 0, 0)),
                  pl.BlockSpec((3, width), lambda b, t: (0, 0))],
        out_specs=[pl.BlockSpec((rows_blk, width), lambda b, t: (b * nt + t, 0)),
                   pl.BlockSpec((1, 2, width), lambda b, t: (b, 0, 0))],
        out_shape=[jax.ShapeDtypeStruct((batch * rows_per_seq, width), BF16),
                   jax.ShapeDtypeStruct((batch, 2, width), F32)],
        scratch_shapes=[pltpu.VMEM((2, width), F32)],
        compiler_params=_params(("parallel", "arbitrary"), rows_blk * width * 40),
        name="short_conv",
    )(z, z, z, conv_buf, conv_w)


def _cumsum_rows(x):
    rows = x.shape[0]
    row = lax.broadcasted_iota(jnp.int32, x.shape, 0)
    shift = 1
    while shift < rows:
        x = x + jnp.where(row >= shift, pltpu.roll(x, shift, 0), 0.0)
        shift *= 2
    return x


def _log_sigmoid(x):
    return jnp.minimum(x, 0.0) - jnp.log1p(jnp.exp(-jnp.abs(x)))


def _mlstm_kernel(q_ref, k_ref, v_ref, og_ref, g_ref, gbias_ref, gain_ref, c0_ref, n0_ref, m0_ref,
                  y_ref, c_ref, n_ref, m_ref, *, valid, heads, dk, dv):
    chunk = pl.program_id(1)
    L = CHUNK_ROWS
    lb = q_ref.shape[0]

    @pl.when(chunk == 0)
    def _():
        c_ref[...] = c0_ref[...]
        n_ref[...] = n0_ref[...]
        m_ref[...] = m0_ref[...]

    row = lax.broadcasted_iota(jnp.int32, (L, 1), 0)
    tri = lax.broadcasted_iota(jnp.int32, (L, L), 1) <= lax.broadcasted_iota(jnp.int32, (L, L), 0)
    row_ok = row < valid
    gates = _pad_rows(g_ref[...], L) + gbias_ref[...]
    ig_all = jnp.where(row_ok, gates, NEG)
    lf_all = jnp.where(row_ok, _log_sigmoid(gates), 0.0)
    b_all = _cumsum_rows(lf_all)
    ig_all_t = ig_all.T
    b_all_t = b_all.T
    scale = dk ** -0.5
    ys = []
    for h in range(heads):
        q = _pad_rows(q_ref[:, h * dk:(h + 1) * dk], L)
        k = _pad_rows(k_ref[:, h * dk:(h + 1) * dk], L)
        v = _pad_rows(v_ref[:, h * dv:(h + 1) * dv], L)
        ig_c = ig_all[:, h:h + 1]
        b_c = b_all[:, heads + h:heads + h + 1]
        ig_r = ig_all_t[h:h + 1, :]
        b_r = b_all_t[heads + h:heads + h + 1, :]
        c0 = c_ref[0, h]
        n0 = n_ref[0, h:h + 1, :]
        m0 = m_ref[0, :, h:h + 1]

        dmat = jnp.where(tri, b_c + (ig_r - b_r), NEG)
        inter = b_c + m0
        m_c = jnp.maximum(inter, jnp.max(dmat, axis=-1, keepdims=True))
        w_inter = jnp.exp(inter - m_c)
        qk = _nt_dot(q, k) * scale
        w_intra = jnp.exp(dmat - m_c) * qk
        num = w_inter * jnp.dot(q, c0.astype(BF16), preferred_element_type=F32)
        num = num + jnp.dot(w_intra.astype(BF16), v, preferred_element_type=F32)
        qn0 = jnp.sum(q.astype(F32) * n0, axis=-1, keepdims=True)
        den = w_inter * qn0 + jnp.sum(w_intra, axis=-1, keepdims=True)
        hid = num / jnp.maximum(jnp.abs(den), jnp.exp(-m_c))

        b_end = b_c[valid - 1:valid, :]
        m_end = m_c[valid - 1:valid, :]
        g_state = jnp.exp(b_end + m0 - m_end)
        g_tok = jnp.exp(b_end - b_c + ig_c - m_end) * scale
        kg = k.astype(F32) * g_tok
        c_ref[0, h] = g_state * c0 + jnp.dot(kg.T.astype(BF16), v, preferred_element_type=F32)
        n_ref[0, h:h + 1, :] = g_state * n0 + jnp.sum(kg, axis=0, keepdims=True)
        m_ref[0, :, h:h + 1] = m_end

        hn = _group_rms(hid, gain_ref[:, h * dv:(h + 1) * dv], dv)
        og = _pad_rows(og_ref[:, h * dv:(h + 1) * dv], L).astype(F32)
        ys.append(hn * jax.nn.sigmoid(og))
    y = jnp.concatenate(ys, axis=1)
    y_ref[...] = y[:lb].astype(y_ref.dtype)


def mlstm_mixer(z, gates, gate_bias, gain, c0, n0, m0, *, batch, rows_per_seq, valid_rows, cols):
    _, heads, dk, dv = c0.shape
    lb = min(CHUNK_ROWS, rows_per_seq)
    nc = rows_per_seq // lb
    assert rows_per_seq % lb == 0
    valid = valid_rows - (nc - 1) * lb
    assert (nc == 1 or valid == lb) and 1 <= valid <= lb
    q_off, k_off, v_off, o_off = cols
    wq, wv = heads * dk, heads * dv

    def zspec(width, off):
        assert off % width == 0
        return pl.BlockSpec((lb, width), lambda b, c, blk=off // width: (b * nc + c, blk))

    m0 = m0.reshape(batch, 1, heads)
    y, c, n, m = pl.pallas_call(
        functools.partial(_mlstm_kernel, valid=valid, heads=heads, dk=dk, dv=dv),
        grid=(batch, nc),
        in_specs=[zspec(wq, q_off), zspec(wq, k_off), zspec(wv, v_off), zspec(wv, o_off),
                  pl.BlockSpec((lb, V7X_LANES), lambda b, c: (b * nc + c, 0)),
                  pl.BlockSpec((1, V7X_LANES), lambda b, c: (0, 0)),
                  pl.BlockSpec((1, wv), lambda b, c: (0, 0)),
                  pl.BlockSpec((1, heads, dk, dv), lambda b, c: (b, 0, 0, 0)),
                  pl.BlockSpec((1, heads, dk), lambda b, c: (b, 0, 0)),
                  pl.BlockSpec((1, 1, heads), lambda b, c: (b, 0, 0))],
        out_specs=[pl.BlockSpec((lb, wv), lambda b, c: (b * nc + c, 0)),
                   pl.BlockSpec((1, heads, dk, dv), lambda b, c: (b, 0, 0, 0)),
                   pl.BlockSpec((1, heads, dk), lambda b, c: (b, 0, 0)),
                   pl.BlockSpec((1, 1, heads), lambda b, c: (b, 0, 0))],
        out_shape=[jax.ShapeDtypeStruct((batch * rows_per_seq, wv), BF16),
                   jax.ShapeDtypeStruct(c0.shape, F32),
                   jax.ShapeDtypeStruct(n0.shape, F32),
                   jax.ShapeDtypeStruct((batch, 1, heads), F32)],
        compiler_params=_params(("parallel", "arbitrary"), 16 << 20),
        name="mlstm",
    )(z, z, z, z, gates, gate_bias, gain.reshape(1, wv), c0, n0, m0)
    return y, c, n, m.reshape(batch, heads)


def _retention_kernel(q_ref, k_ref, v_ref, gd_ref, gain_ref, s0_ref, y_ref, s_ref, *, valid, heads, dk, dv):
    chunk = pl.program_id(1)
    L = CHUNK_ROWS
    lb = q_ref.shape[0]

    @pl.when(chunk == 0)
    def _():
        s_ref[...] = s0_ref[...]

    row = lax.broadcasted_iota(jnp.int32, (L, 1), 0)
    t_idx = lax.broadcasted_iota(jnp.int32, (L, L), 0)
    s_idx = lax.broadcasted_iota(jnp.int32, (L, L), 1)
    dist = (t_idx - s_idx).astype(F32)
    keep = (t_idx >= s_idx) & (s_idx < valid)
    rowf = row.astype(F32)
    scale = dk ** -0.5
    ys = []
    for h in range(heads):
        log_gamma = math.log1p(-(2.0 ** (-5.0 - h)))
        q = _pad_rows(q_ref[:, h * dk:(h + 1) * dk], L)
        k = _pad_rows(k_ref[:, h * dk:(h + 1) * dk], L)
        v = _pad_rows(v_ref[:, h * dv:(h + 1) * dv], L)
        s0 = s_ref[0, h]
        decay = jnp.where(keep, jnp.exp(jnp.maximum(dist, 0.0) * log_gamma), 0.0)
        smat = _nt_dot(q, k) * scale * decay
        inter = jnp.exp((rowf + 1.0) * log_gamma)
        o = jnp.dot(smat.astype(BF16), v, preferred_element_type=F32)
        o = o + inter * jnp.dot(q, s0.astype(BF16), preferred_element_type=F32)
        tail = jnp.where(row < valid, jnp.exp((valid - 1.0 - rowf) * log_gamma), 0.0) * scale
        kt = k.astype(F32) * tail
        s_ref[0, h] = math.exp(valid * log_gamma) * s0 + jnp.dot(kt.T.astype(BF16), v, preferred_element_type=F32)
        on = _group_rms(o, gain_ref[:, h * dv:(h + 1) * dv], dv)
        gd = _pad_rows(gd_ref[:, h * dv:(h + 1) * dv], L).astype(F32)
        ys.append(on * (gd * jax.nn.sigmoid(gd)))
    y = jnp.concatenate(ys, axis=1)
    y_ref[...] = y[:lb].astype(y_ref.dtype)


def retention_mixer(z, gain, s0, *, batch, rows_per_seq, valid_rows, cols):
    _, heads, dk, dv = s0.shape
    lb = min(CHUNK_ROWS, rows_per_seq)
    nc = rows_per_seq // lb
    valid = valid_rows - (nc - 1) * lb
    assert (nc == 1 or valid == lb) and 1 <= valid <= lb
    q_off, k_off, v_off, g_off = cols
    wq, wv = heads * dk, heads * dv

    def zspec(width, off):
        assert off % width == 0
        return pl.BlockSpec((lb, width), lambda b, c, blk=off // width: (b * nc + c, blk))

    return pl.pallas_call(
        functools.partial(_retention_kernel, valid=valid, heads=heads, dk=dk, dv=dv),
        grid=(batch, nc),
        in_specs=[zspec(wq, q_off), zspec(wq, k_off), zspec(wv, v_off), zspec(wv, g_off),
                  pl.BlockSpec((1, wv), lambda b, c: (0, 0)),
                  pl.BlockSpec((1, heads, dk, dv), lambda b, c: (b, 0, 0, 0))],
        out_specs=[pl.BlockSpec((lb, wv), lambda b, c: (b * nc + c, 0)),
                   pl.BlockSpec((1, heads, dk, dv), lambda b, c: (b, 0, 0, 0))],
        out_shape=[jax.ShapeDtypeStruct((batch * rows_per_seq, wv), BF16),
                   jax.ShapeDtypeStruct(s0.shape, F32)],
        compiler_params=_params(("parallel", "arbitrary"), 16 << 20),
        name="retention",
    )(z, z, z, z, gain.reshape(1, wv), s0)


def _store_cache_rows(ref, x, heads):
    rows, width = x.shape
    dv = width // heads
    tiles_per_head = dv // V7X_LANES
    step = heads * tiles_per_head
    for h in range(heads):
        for t in range(tiles_per_head):
            c0 = h * dv + t * V7X_LANES
            ref[pl.ds(t * heads + h, rows, stride=step), :] = x[:, c0:c0 + V7X_LANES]


def _load_cache_head(ref3, h, heads, tokens, dv):
    tiles_per_head = dv // V7X_LANES
    step = heads * tiles_per_head
    return jnp.concatenate([ref3[0, pl.ds(t * heads + h, tokens, stride=step), :]
                            for t in range(tiles_per_head)], axis=1)


def _cache_view(a, lead):
    tokens, heads, dv = a.shape[-3:]
    t = dv // V7X_LANES
    return (a.reshape(lead, tokens, heads, t, V7X_LANES).transpose(0, 1, 3, 2, 4)
            .reshape(lead, tokens * t * heads, V7X_LANES))


def _cache_unview(a, batch, tokens, heads, dv):
    t = dv // V7X_LANES
    return (a.reshape(batch, tokens, t, heads, V7X_LANES).transpose(0, 1, 3, 2, 4)
            .reshape(batch, tokens, heads, dv))


def _diff_prep_kernel(q_ref, k_ref, v_ref, gq_ref, gk_ref, qn_ref, kb_ref, kc_ref, vc_ref, *, dh, heads):
    qn_ref[...] = _group_rms(q_ref[...].astype(F32), gq_ref[...], dh).astype(qn_ref.dtype)
    kn = _group_rms(k_ref[...].astype(F32), gk_ref[...], dh)
    kb_ref[...] = kn.astype(kb_ref.dtype)
    _store_cache_rows(kc_ref, kn, heads)
    _store_cache_rows(vc_ref, v_ref[...].astype(F32), heads)


def diff_prep(z, g_q, g_k, *, rows_blk, cols, width, heads):
    m = z.shape[0]
    dh = g_q.shape[0]
    rows_blk = min(rows_blk, m)
    q_off, k_off, v_off = cols
    il = width // V7X_LANES

    def zspec(off):
        assert off % width == 0
        return pl.BlockSpec((rows_blk, width), lambda i, blk=off // width: (i, blk))

    ospec = pl.BlockSpec((rows_blk, width), lambda i: (i, 0))
    cspec = pl.BlockSpec((rows_blk * il, V7X_LANES), lambda i: (i, 0))
    return pl.pallas_call(
        functools.partial(_diff_prep_kernel, dh=dh, heads=heads),
        grid=(m // rows_blk,),
        in_specs=[zspec(q_off), zspec(k_off), zspec(v_off),
                  pl.BlockSpec((1, dh), lambda i: (0, 0)), pl.BlockSpec((1, dh), lambda i: (0, 0))],
        out_specs=[ospec, ospec, cspec, cspec],
        out_shape=[jax.ShapeDtypeStruct((m, width), BF16),
                   jax.ShapeDtypeStruct((m, width), BF16),
                   jax.ShapeDtypeStruct((m * il, V7X_LANES), F32),
                   jax.ShapeDtypeStruct((m * il, V7X_LANES), F32)],
        compiler_params=_params(("parallel",), rows_blk * width * 48),
        name="diff_prep",
    )(z, z, z, g_q.reshape(1, dh), g_k.reshape(1, dh))


def _softmax_update(s, v, m_ref, l_ref, acc_ref):
    m_old = m_ref[...]
    m_new = jnp.maximum(m_old, jnp.max(s, axis=-1, keepdims=True))
    alpha = jnp.exp(m_old - m_new)
    p = jnp.exp(s - m_new)
    l_ref[...] = alpha * l_ref[...] + jnp.sum(p, axis=-1, keepdims=True)
    acc_ref[...] = alpha * acc_ref[...] + jnp.dot(p.astype(BF16), v, preferred_element_type=F32)
    m_ref[...] = m_new


def _diff_attn_prompt_kernel(slope_ref, lam_ref, q_ref, k_ref, v_ref, gain_ref, y_ref, m_s, l_s, acc_s,
                             *, bq, bk, dh, out_scale):
    h = pl.program_id(1)
    qi = pl.program_id(2)
    ki = pl.program_id(3)

    @pl.when(ki == 0)
    def _():
        m_s[...] = jnp.full(m_s.shape, NEG, F32)
        l_s[...] = jnp.zeros(l_s.shape, F32)
        acc_s[...] = jnp.zeros(acc_s.shape, F32)

    @pl.when(ki <= qi)
    def _():
        q = q_ref[...]
        k = k_ref[...]
        v = v_ref[...]
        rel = lax.broadcasted_iota(jnp.int32, (1, bk), 1) + (ki * bk - qi * bq)
        bias = slope_ref[h] * rel.astype(F32)
        qrow = lax.broadcasted_iota(jnp.int32, (bq, 1), 0)
        masked_bias = jnp.where(rel <= qrow, bias, NEG)
        for j in range(2):
            s = _nt_dot(q[:, j * dh:(j + 1) * dh], k[:, j * dh:(j + 1) * dh]) * dh ** -0.5 + masked_bias
            _softmax_update(s, v, m_s.at[j], l_s.at[j], acc_s.at[j])

    @pl.when(ki == qi)
    def _():
        o = acc_s[0] / l_s[0] - lam_ref[0] * (acc_s[1] / l_s[1])
        y_ref[...] = (_group_rms(o, gain_ref[...], o.shape[1]) * out_scale).astype(y_ref.dtype)


def diff_attention_prompt(qn, kn, z, slopes, lam, gain, *, batch, seq, heads, dh, dv, v_off, blk, out_scale):
    bq = bk = min(blk, seq)
    nq = seq // bq
    hw = 2 * dh
    assert hw == dv and v_off % dv == 0
    vblk = v_off // dv

    def kv_row(b, qi, ki):
        return b * nq + jnp.minimum(ki, qi)

    grid_spec = pltpu.PrefetchScalarGridSpec(
        num_scalar_prefetch=0,
        grid=(batch, heads, nq, nq),
        in_specs=[pl.BlockSpec(memory_space=pltpu.SMEM),
                  pl.BlockSpec(memory_space=pltpu.SMEM),
                  pl.BlockSpec((bq, hw), lambda b, h, qi, ki: (b * nq + qi, h)),
                  pl.BlockSpec((bk, hw), lambda b, h, qi, ki: (kv_row(b, qi, ki), h)),
                  pl.BlockSpec((bk, dv), lambda b, h, qi, ki: (kv_row(b, qi, ki), vblk + h)),
                  pl.BlockSpec((1, dv), lambda b, h, qi, ki: (0, h))],
        out_specs=pl.BlockSpec((bq, dv), lambda b, h, qi, ki: (b * nq + qi, h)),
        scratch_shapes=[pltpu.VMEM((2, bq, 1), F32), pltpu.VMEM((2, bq, 1), F32), pltpu.VMEM((2, bq, dv), F32)],
    )
    return pl.pallas_call(
        functools.partial(_diff_attn_prompt_kernel, bq=bq, bk=bk, dh=dh, out_scale=out_scale),
        grid_spec=grid_spec,
        out_shape=jax.ShapeDtypeStruct((batch * seq, heads * dv), BF16),
        compiler_params=_params(("parallel", "parallel", "parallel", "arbitrary"), 24 << 20),
        name="diff_attn_prompt",
    )(slopes, lam, qn, kn, z, gain.reshape(1, heads * dv))


def _diff_attn_decode_kernel(pt_ref, slope_ref, lam_ref, q_ref, kn_ref, vn_ref, gain_ref, *rest,
                             pages, page, past, valid, heads, dh, dv, out_scale):
    k_refs = rest[:pages]
    v_refs = rest[pages:2 * pages]
    y_ref, m_s, l_s, acc_s = rest[2 * pages:]
    step = pl.program_id(1)
    nsteps = pl.num_programs(1)
    rows = q_ref.shape[0]
    hw = 2 * dh

    @pl.when(step == 0)
    def _():
        m_s[...] = jnp.full(m_s.shape, NEG, F32)
        l_s[...] = jnp.zeros(l_s.shape, F32)
        acc_s[...] = jnp.zeros(acc_s.shape, F32)

    def stacked_queries(h):
        qh = q_ref[:, h * hw:(h + 1) * hw]
        zero = jnp.zeros((rows, dh), qh.dtype)
        return jnp.concatenate([jnp.concatenate([qh[:, :dh], zero], axis=1),
                                jnp.concatenate([zero, qh[:, dh:]], axis=1)], axis=0)

    nkeys = pages * page
    rel = lax.broadcasted_iota(jnp.int32, (1, nkeys), 1) + (step * nkeys - past)
    for h in range(heads):
        qs = stacked_queries(h)
        k = jnp.concatenate([_load_cache_head(r, h, heads, page, hw) for r in k_refs], axis=0).astype(BF16)
        v = jnp.concatenate([_load_cache_head(r, h, heads, page, dv) for r in v_refs], axis=0).astype(BF16)
        s = _nt_dot(qs, k) * dh ** -0.5 + slope_ref[h] * rel.astype(F32)
        _softmax_update(s, v, m_s.at[h], l_s.at[h], acc_s.at[h])

    @pl.when(step == nsteps - 1)
    def _():
        nk = V7X_LANES
        key = lax.broadcasted_iota(jnp.int32, (1, nk), 1)
        tok = lax.broadcasted_iota(jnp.int32, (2 * rows, 1), 0) & (rows - 1)
        ok = (key <= tok) & (key < valid)
        for h in range(heads):
            qs = stacked_queries(h)
            k = _pad_rows(kn_ref[:, h * hw:(h + 1) * hw], nk)
            v = _pad_rows(vn_ref[:, h * dv:(h + 1) * dv], nk)
            s = _nt_dot(qs, k) * dh ** -0.5 + slope_ref[h] * key.astype(F32)
            s = jnp.where(ok, s, NEG)
            _softmax_update(s, v, m_s.at[h], l_s.at[h], acc_s.at[h])
            o = acc_s[h] / l_s[h]
            o = o[:rows] - lam_ref[0] * o[rows:]
            y_ref[:, h * dv:(h + 1) * dv] = (
                _group_rms(o, gain_ref[:, h * dv:(h + 1) * dv], dv) * out_scale).astype(y_ref.dtype)


def diff_attention_decode(qn, kn, z, cache_k, cache_v, page_table, layer, slopes, lam, gain,
                          *, batch, rows_per_seq, valid_rows, heads, dh, dv, v_off, out_scale):
    depth, n_pool, page = cache_k.shape[:3]
    n_pages = page_table.shape[1]
    pages = PAGES_PER_STEP
    assert n_pages % pages == 0
    nsteps = n_pages // pages
    width = heads * dv
    ck = _cache_view(cache_k, depth * n_pool)
    cv = _cache_view(cache_v, depth * n_pool)
    base = layer * n_pool
    page_rows = page * width // V7X_LANES

    def page_spec(p):
        return pl.BlockSpec((1, page_rows, V7X_LANES),
                            lambda b, s, pt, p=p: (base + pt[b * n_pages + s * pages + p], 0, 0))

    row_spec = pl.BlockSpec((rows_per_seq, width), lambda b, s, pt: (b, 0))
    grid_spec = pltpu.PrefetchScalarGridSpec(
        num_scalar_prefetch=1,
        grid=(batch, nsteps),
        in_specs=[pl.BlockSpec(memory_space=pltpu.SMEM),
                  pl.BlockSpec(memory_space=pltpu.SMEM),
                  row_spec, row_spec,
                  pl.BlockSpec((rows_per_seq, width), lambda b, s, pt: (b, v_off // width)),
                  pl.BlockSpec((1, width), lambda b, s, pt: (0, 0))]
                 + [page_spec(p) for p in range(pages)] + [page_spec(p) for p in range(pages)],
        out_specs=row_spec,
        scratch_shapes=[pltpu.VMEM((heads, 2 * rows_per_seq, 1), F32),
                        pltpu.VMEM((heads, 2 * rows_per_seq, 1), F32),
                        pltpu.VMEM((heads, 2 * rows_per_seq, dv), F32)],
    )
    assert v_off % width == 0
    return pl.pallas_call(
        functools.partial(_diff_attn_decode_kernel, pages=pages, page=page, past=n_pages * page,
                          valid=valid_rows, heads=heads, dh=dh, dv=dv, out_scale=out_scale),
        grid_spec=grid_spec,
        out_shape=jax.ShapeDtypeStruct((batch * rows_per_seq, width), BF16),
        compiler_params=_params(("parallel", "arbitrary"), 4 * pages * page * width * 4 + (8 << 20)),
        name="diff_attn_decode",
    )(page_table.reshape(-1), slopes, lam, qn, kn, z, gain.reshape(1, width),
      *([ck] * pages), *([cv] * pages))


def _cross_attn_kernel(q_ref, mk_ref, mv_ref, o_ref, *, heads, dh, n_mem, cache_layout):
    outs = []
    for h in range(heads):
        q = q_ref[:, h * dh:(h + 1) * dh]
        if cache_layout:
            mk = _load_cache_head(mk_ref, h, heads, n_mem, dh).astype(BF16)
            mv = _load_cache_head(mv_ref, h, heads, n_mem, dh).astype(BF16)
        else:
            mk = mk_ref[:, h * dh:(h + 1) * dh].astype(BF16)
            mv = mv_ref[:, h * dh:(h + 1) * dh].astype(BF16)
        s = _nt_dot(q, mk) * dh ** -0.5
        p = jnp.exp(s - jnp.max(s, axis=-1, keepdims=True))
        o = jnp.dot(p.astype(BF16), mv, preferred_element_type=F32)
        outs.append(o / jnp.sum(p, axis=-1, keepdims=True))
    o_ref[...] = jnp.concatenate(outs, axis=1).astype(o_ref.dtype)


def cross_attention(qx, mk, mv, *, batch, rows_per_seq, n_mem, heads, dh, rows_blk, cache_layer=None):
    width = heads * dh
    rows_blk = min(rows_blk, rows_per_seq)
    nq = rows_per_seq // rows_blk
    if cache_layer is None:
        mem_spec = pl.BlockSpec((n_mem, width), lambda b, i: (b, 0))
    else:
        lead = mk.shape[0] * mk.shape[1]
        mk, mv = _cache_view(mk, lead), _cache_view(mv, lead)
        mem_spec = pl.BlockSpec((1, n_mem * width // V7X_LANES, V7X_LANES),
                                lambda b, i: (cache_layer * batch + b, 0, 0))
    return pl.pallas_call(
        functools.partial(_cross_attn_kernel, heads=heads, dh=dh, n_mem=n_mem,
                          cache_layout=cache_layer is not None),
        grid=(batch, nq),
        in_specs=[pl.BlockSpec((rows_blk, width), lambda b, i: (b * nq + i, 0)), mem_spec, mem_spec],
        out_specs=pl.BlockSpec((rows_blk, width), lambda b, i: (b * nq + i, 0)),
        out_shape=jax.ShapeDtypeStruct((batch * rows_per_seq, width), BF16),
        compiler_params=_params(("parallel", "parallel"), 24 << 20),
        name="cross_attn",
    )(qx, mk, mv)


def kernel(x_prompt, x_sample, cache_diff_k, cache_diff_v, cache_mem_k, cache_mem_v, state_conv, state_mlstm_C, state_mlstm_n, state_mlstm_m, state_ret, page_table, mem_prompt, norm_mix, w_in, conv_w, b_igate, b_fgate, g_mlstm, g_diff_q, g_diff_k, lam_q1, lam_k1, lam_q2, lam_k2, g_diff_out, g_ret, w_out, norm_xattn, norm_mem, w_xq, w_xk, w_xv, g_xq, g_xk, w_xo, norm_mlp, w_up, w_down):
    depth = w_in.shape[0]
    bp, seq, d_model = x_prompt.shape
    bs, dec = x_sample.shape[:2]
    conv_ch = conv_w.shape[2]
    _, _, hb, dkb, dvb = state_mlstm_C.shape
    _, _, hd, dkd, dvd = state_ret.shape
    hc, dvc = cache_diff_v.shape[3:]
    dhc = dvc // 2
    _, _, n_mem, hx, dhx = cache_mem_k.shape
    xw = hx * dhx
    assert seq % CHUNK_ROWS == 0 and dec <= SAMPLE_ROWS

    sizes = (conv_ch, conv_ch, conv_ch, hb * dkb, hb * dkb, hb * dvb, hb * dvb, hb, hb,
             2 * hc * dhc, 2 * hc * dhc, hc * dvc, hd * dkd, hd * dkd, hd * dvd, hd * dvd)
    names = ("a_x", "a_b", "a_c", "q_b", "k_b", "v_b", "o_b", "i_b", "f_b",
             "q_c", "k_c", "v_c", "q_d", "k_d", "v_d", "g_d")
    off, pos = {}, 0
    for nm, sz in zip(names, sizes):
        if nm not in ("i_b", "f_b"):
            off[nm] = pos
            pos += sz

    slopes = (2.0 ** (-8.0 * jnp.arange(1, hc + 1, dtype=F32) / hc)).astype(F32)

    x_p = x_prompt.reshape(bp * seq, d_model)
    x_s = jnp.pad(x_sample, ((0, 0), (0, SAMPLE_ROWS - dec), (0, 0))).reshape(bs * SAMPLE_ROWS, d_model)
    mem_rows = mem_prompt.reshape(bp * n_mem, d_model)

    groups = (
        dict(batch=bp, rows=seq, valid=seq, tm=1024, prompt=True),
        dict(batch=bs, rows=SAMPLE_ROWS, valid=dec, tm=bs * SAMPLE_ROWS, prompt=False),
    )
    xs = [x_p, x_s]
    outs = [dict(conv=[], C=[], n=[], m=[], k=[], v=[], S=[], mk=[], mv=[]) for _ in groups]

    gate_lo = sum(sizes[:7])
    w_main, w_gate = split_w_in(w_in, gate_lo, gate_lo + 2 * hb)
    w_out_b, w_xq_b, w_xk_b, w_xv_b, w_xo_b, w_up_b, w_down_b = (
        cast_bf16(w) for w in (w_out, w_xq, w_xk, w_xv, w_xo, w_up, w_down))

    for li in range(depth):
        lam_init = 0.8 - 0.6 * math.exp(-0.3 * li)
        lam = (jnp.exp(jnp.sum(lam_q1[li] * lam_k1[li])) - jnp.exp(jnp.sum(lam_q2[li] * lam_k2[li]))
               + lam_init).reshape(1).astype(F32)
        gate_bias = jnp.pad(jnp.concatenate([b_igate[li], b_fgate[li]]), (0, V7X_LANES - 2 * hb)).reshape(1, V7X_LANES)

        hm = rmsnorm_rows(mem_rows, norm_mem[li], 256)
        mk_p = matmul([hm], w_xk_b, li, tm=1024, tn=1024, out_dtype=F32,
                      epilogue="headnorm", extra=g_xk[li], group=dhx, name="mm_mem_k")
        mv_p = matmul([hm], w_xv_b, li, tm=1024, tn=1024, out_dtype=F32, name="mm_mem_v")

        for gi, g in enumerate(groups):
            x = xs[gi]
            batch, rows, valid, tm = g["batch"], g["rows"], g["valid"], g["tm"]
            if g["prompt"]:
                conv_buf = jnp.zeros((batch, 2, conv_ch), F32)
                c0 = jnp.zeros((batch, hb, dkb, dvb), F32)
                n0 = jnp.zeros((batch, hb, dkb), F32)
                m0 = jnp.zeros((batch, hb), F32)
                s0 = jnp.zeros((batch, hd, dkd, dvd), F32)
                mk, mv, cache_layer = mk_p, mv_p, None
            else:
                conv_buf, c0, n0, m0, s0 = (state_conv[li], state_mlstm_C[li], state_mlstm_n[li],
                                            state_mlstm_m[li], state_ret[li])
                mk, mv, cache_layer = cache_mem_k, cache_mem_v, li

            h = rmsnorm_rows(x, norm_mix[li], 256)
            z = matmul([h], w_main, li, tm=tm, tn=1024, name="mm_in")
            gates = matmul([h], w_gate, li, tm=tm, tn=V7X_LANES, out_dtype=F32, name="mm_gate")
            y_a, conv_new = short_conv(z, conv_buf, conv_w[li], batch=batch, rows_per_seq=rows, rows_blk=512,
                                       valid_rows=valid,
                                       col_blocks=(off["a_x"] // conv_ch, off["a_b"] // conv_ch, off["a_c"] // conv_ch))
            y_b, c_new, n_new, m_new = mlstm_mixer(
                z, gates, gate_bias, g_mlstm[li], c0, n0, m0, batch=batch, rows_per_seq=rows, valid_rows=valid,
                cols=(off["q_b"], off["k_b"], off["v_b"], off["o_b"]))
            qn, kb, kc, vc = diff_prep(z, g_diff_q[li], g_diff_k[li], rows_blk=512,
                                       cols=(off["q_c"], off["k_c"], off["v_c"]), width=hc * dvc, heads=hc)
            if g["prompt"]:
                y_c = diff_attention_prompt(qn, kb, z, slopes, lam, g_diff_out[li], batch=batch, seq=rows,
                                            heads=hc, dh=dhc, dv=dvc, v_off=off["v_c"], blk=512,
                                            out_scale=1.0 - lam_init)
            else:
                y_c = diff_attention_decode(qn, kb, z, cache_diff_k, cache_diff_v, page_table, li, slopes, lam,
                                            g_diff_out[li], batch=batch, rows_per_seq=rows, valid_rows=valid,
                                            heads=hc, dh=dhc, dv=dvc, v_off=off["v_c"], out_scale=1.0 - lam_init)
            y_d, s_new = retention_mixer(z, g_ret[li], s0, batch=batch, rows_per_seq=rows, valid_rows=valid,
                                         cols=(off["q_d"], off["k_d"], off["v_d"], off["g_d"]))
            x = matmul([y_a, y_b, y_c, y_d], w_out_b, li, tm=tm, tn=512, out_dtype=F32,
                       epilogue="residual", extra=x, name="mm_out")

            h2 = rmsnorm_rows(x, norm_xattn[li], 256)
            qx = matmul([h2], w_xq_b, li, tm=tm, tn=1024, epilogue="headnorm", extra=g_xq[li], group=dhx,
                        name="mm_xq")
            ox = cross_attention(qx, mk, mv, batch=batch, rows_per_seq=rows, n_mem=n_mem, heads=hx, dh=dhx,
                                 rows_blk=512, cache_layer=cache_layer)
            x = matmul([ox], w_xo_b, li, tm=tm, tn=1024, out_dtype=F32, epilogue="residual", extra=x, name="mm_xo")

            h3 = rmsnorm_rows(x, norm_mlp[li], 256)
            hid = matmul([h3], w_up_b, li, tm=tm, tn=1024, epilogue="relu2", name="mm_up")
            x = matmul([hid], w_down_b, li, tm=tm, tn=1024, tk=2048, out_dtype=F32,
                       epilogue="residual", extra=x, name="mm_down")

            xs[gi] = x
            o = outs[gi]
            o["conv"].append(conv_new); o["C"].append(c_new); o["n"].append(n_new); o["m"].append(m_new)
            o["k"].append(kc); o["v"].append(vc); o["S"].append(s_new)
            if g["prompt"]:
                o["mk"].append(mk_p); o["mv"].append(mv_p)

    op, os_ = outs
    y_prompt = xs[0].reshape(bp, seq, d_model)
    y_sample = xs[1].reshape(bs, SAMPLE_ROWS, d_model)[:, :dec]

    def cache_rows(parts, batch, rows, valid):
        a = jnp.stack([_cache_unview(p, batch, rows, hc, dvc) for p in parts])
        return a if rows == valid else a[:, :, :valid]

    return (y_prompt, y_sample,
            jnp.stack(op["conv"]), jnp.stack(op["C"]), jnp.stack(op["n"]), jnp.stack(op["m"]),
            cache_rows(op["k"], bp, seq, seq), cache_rows(op["v"], bp, seq, seq), jnp.stack(op["S"]),
            jnp.stack(op["mk"]).reshape(depth, bp, n_mem, hx, dhx),
            jnp.stack(op["mv"]).reshape(depth, bp, n_mem, hx, dhx),
            jnp.stack(os_["conv"]), jnp.stack(os_["C"]), jnp.stack(os_["n"]), jnp.stack(os_["m"]),
            cache_rows(os_["k"], bs, SAMPLE_ROWS, dec), cache_rows(os_["v"], bs, SAMPLE_ROWS, dec),
            jnp.stack(os_["S"]))
```

```python
import functools
import math

import jax
import jax.numpy as jnp
from jax import lax
from jax.experimental import pallas as pl
from jax.experimental.pallas import tpu as pltpu

V7X_LANES = 128
V7X_VMEM_BYTES = 64 * 1024 * 1024
VMEM_LIMIT_CAP = V7X_VMEM_BYTES - 6 * 1024 * 1024

EPS = 1e-6
NEG = -1e30
CHUNK_ROWS = 128
SAMPLE_ROWS = 16
PAGES_PER_STEP = 8

BF16 = jnp.bfloat16
F32 = jnp.float32


def _params(semantics, vmem_estimate):
    limit = int(min(max(vmem_estimate * 5 // 4 + (4 << 20), 32 << 20), VMEM_LIMIT_CAP))
    return pltpu.CompilerParams(dimension_semantics=semantics, vmem_limit_bytes=limit)


def _nt_dot(a, b):
    return lax.dot_general(a, b, (((1,), (1,)), ((), ())), preferred_element_type=F32)


def _pad_rows(x, rows):
    if x.shape[0] == rows:
        return x
    return jnp.concatenate([x, jnp.zeros((rows - x.shape[0],) + x.shape[1:], x.dtype)], axis=0)


def _group_rms(x, gain, width):
    parts = []
    for g in range(x.shape[1] // width):
        blk = x[:, g * width:(g + 1) * width]
        ms = jnp.mean(blk * blk, axis=-1, keepdims=True)
        parts.append(blk * lax.rsqrt(ms + EPS) * gain)
    return parts[0] if len(parts) == 1 else jnp.concatenate(parts, axis=1)


def _rmsnorm_kernel(x_ref, g_ref, o_ref):
    x = x_ref[...]
    ms = jnp.mean(x * x, axis=-1, keepdims=True)
    o_ref[...] = (x * lax.rsqrt(ms + EPS) * g_ref[...]).astype(o_ref.dtype)


def rmsnorm_rows(x, g, rows_blk):
    m, d = x.shape
    rows_blk = min(rows_blk, m)
    return pl.pallas_call(
        _rmsnorm_kernel,
        grid=(m // rows_blk,),
        in_specs=[pl.BlockSpec((rows_blk, d), lambda i: (i, 0)),
                  pl.BlockSpec((1, d), lambda i: (0, 0))],
        out_specs=pl.BlockSpec((rows_blk, d), lambda i: (i, 0)),
        out_shape=jax.ShapeDtypeStruct((m, d), BF16),
        compiler_params=_params(("parallel",), rows_blk * d * 12),
        name="rmsnorm",
    )(x, g.reshape(1, d))


def _cast_kernel(x_ref, o_ref):
    o_ref[...] = x_ref[...].astype(o_ref.dtype)


def cast_bf16(w, rows_blk=256, cols_blk=4096):
    d, k, n = w.shape
    tr, tc = min(rows_blk, k), min(cols_blk, n)
    assert k % tr == 0 and n % tc == 0
    spec = pl.BlockSpec((None, tr, tc), lambda l, i, j: (l, i, j))
    return pl.pallas_call(
        _cast_kernel,
        grid=(d, k // tr, n // tc),
        in_specs=[spec], out_specs=spec,
        out_shape=jax.ShapeDtypeStruct(w.shape, BF16),
        compiler_params=_params(("parallel", "parallel", "parallel"), tr * tc * 16),
        name="cast_bf16",
    )(w)


def _transpose_cast_kernel(x_ref, o_ref):
    o_ref[...] = x_ref[0].T.astype(o_ref.dtype)


def split_w_in(w_in, g0, g1, tn=1024, tk=512):
    d, k, n = w_in.shape
    glen = g1 - g0
    n_main = n - glen
    assert g0 % tn == 0 and n_main % tn == 0 and k % tk == 0 and glen % 8 == 0 and glen <= V7X_LANES
    w_t = jnp.swapaxes(w_in, 1, 2)

    def src_index(l, j, kk):
        row = j * tn + jnp.where(j * tn >= g0, glen, 0)
        return (l, pl.multiple_of(row, 8), pl.multiple_of(kk * tk, V7X_LANES))

    main = pl.pallas_call(
        _transpose_cast_kernel,
        grid=(d, n_main // tn, k // tk),
        in_specs=[pl.BlockSpec((pl.Element(1), pl.Element(tn), pl.Element(tk)), src_index)],
        out_specs=pl.BlockSpec((None, tk, tn), lambda l, j, kk: (l, kk, j)),
        out_shape=jax.ShapeDtypeStruct((d, k, n_main), BF16),
        compiler_params=_params(("parallel", "parallel", "parallel"), tn * tk * 24),
        name="split_w_in",
    )(w_t)
    gate = jnp.pad(w_in[:, :, g0:g1], ((0, 0), (0, 0), (0, V7X_LANES - glen))).astype(BF16)
    return main, gate


def _mm_kernel(*refs, n_a, kt, epilogue, group, side_blocks, steps_jk):
    a_refs = refs[:n_a]
    w_refs = refs[n_a:2 * n_a]
    pos = 2 * n_a
    extra_ref = None
    if epilogue in ("residual", "headnorm"):
        extra_ref = refs[pos]
        pos += 1
    side_in = None
    if side_blocks:
        side_in = refs[pos]
        pos += 1
    o_ref = refs[pos]
    side_out = refs[pos + 1] if side_blocks else None

    if side_blocks:
        step = (pl.program_id(0) * steps_jk[0] + pl.program_id(1)) * steps_jk[1] + pl.program_id(2)

        @pl.when(step < side_blocks)
        def _():
            side_out[...] = side_in[...].astype(side_out.dtype)

    part = jnp.dot(a_refs[0][...], w_refs[0][...], preferred_element_type=F32)
    for a_ref, w_ref in zip(a_refs[1:], w_refs[1:]):
        part = part + jnp.dot(a_ref[...], w_ref[...], preferred_element_type=F32)

    if kt > 1:
        k = pl.program_id(2)

        @pl.when(k == 0)
        def _():
            o_ref[...] = extra_ref[...] + part

        @pl.when(k > 0)
        def _():
            o_ref[...] += part
    elif epilogue == "residual":
        o_ref[...] = extra_ref[...] + part
    elif epilogue == "relu2":
        o_ref[...] = jnp.square(jnp.maximum(part, 0.0)).astype(o_ref.dtype)
    elif epilogue == "headnorm":
        o_ref[...] = _group_rms(part, extra_ref[...], group).astype(o_ref.dtype)
    else:
        o_ref[...] = part.astype(o_ref.dtype)


def matmul(a_list, w, layer, *, tm, tn, tk=None, out_dtype=BF16, epilogue="cast", extra=None, group=None,
           side=None, name="mm"):
    n_a = len(a_list)
    m, kc = a_list[0].shape
    _, k_total, n = w.shape
    assert kc * n_a == k_total
    tm = min(tm, m)
    tn = min(tn, n)
    tk = kc if tk is None else min(tk, kc)
    assert m % tm == 0 and n % tn == 0 and kc % tk == 0
    kt = kc // tk
    assert kt == 1 or (n_a == 1 and epilogue == "residual" and out_dtype == F32)
    kblocks_per_chunk = kc // tk
    grid = (m // tm, n // tn, kt)

    in_specs = []
    for c in range(n_a):
        in_specs.append(pl.BlockSpec((tm, tk), lambda i, j, k: (i, k)))
    for c in range(n_a):
        in_specs.append(pl.BlockSpec((None, tk, tn), lambda i, j, k, c=c: (layer, c * kblocks_per_chunk + k, j)))
    args = list(a_list) + [w] * n_a
    out_bytes = jnp.dtype(out_dtype).itemsize
    vmem = 2 * n_a * (tm * tk + tk * tn) * 2 + 2 * tm * tn * out_bytes + tm * tn * 4
    if epilogue == "residual":
        in_specs.append(pl.BlockSpec((tm, tn), lambda i, j, k: (i, j)))
        args.append(extra)
        vmem += 2 * tm * tn * 4
    elif epilogue == "headnorm":
        assert tn % group == 0
        in_specs.append(pl.BlockSpec((1, group), lambda i, j, k: (0, 0)))
        args.append(extra.reshape(1, group))
    out_specs = [pl.BlockSpec((tm, tn), lambda i, j, k: (i, j))]
    out_shape = [jax.ShapeDtypeStruct((m, n), out_dtype)]
    side_blocks = 0
    semantics = ("parallel", "parallel", "arbitrary")
    if side is not None:
        src, src_layer = side
        _, r, c = src.shape
        steps = grid[0] * grid[1] * grid[2]
        side_blocks = 1 << (steps.bit_length() - 1)
        rows_blk = r // side_blocks
        assert rows_blk * side_blocks == r and rows_blk % 16 == 0

        def side_block(i, j, k):
            return jnp.minimum((i * grid[1] + j) * grid[2] + k, side_blocks - 1)

        in_specs.append(pl.BlockSpec((None, rows_blk, c), lambda i, j, k: (src_layer, side_block(i, j, k), 0)))
        args.append(src)
        out_specs.append(pl.BlockSpec((None, rows_blk, c), lambda i, j, k: (0, side_block(i, j, k), 0)))
        out_shape.append(jax.ShapeDtypeStruct((1, r, c), BF16))
        vmem += 2 * rows_blk * c * 6
        semantics = ("arbitrary", "arbitrary", "arbitrary")
    res = pl.pallas_call(
        functools.partial(_mm_kernel, n_a=n_a, kt=kt, epilogue=epilogue, group=group,
                          side_blocks=side_blocks, steps_jk=grid[1:]),
        grid=grid,
        in_specs=in_specs,
        out_specs=out_specs,
        out_shape=out_shape,
        compiler_params=pltpu.CompilerParams(
            dimension_semantics=semantics,
            vmem_limit_bytes=int(min(max(vmem + (6 << 20), 32 << 20), VMEM_LIMIT_CAP))),
        name=name,
    )(*args)
    return res if side is not None else res[0]


def _conv_kernel(ax_ref, ab_ref, ac_ref, buf_ref, w_ref, y_ref, new_ref, carry_ref, *, last_valid):
    t = pl.program_id(1)
    nt = pl.num_programs(1)
    rows = ax_ref.shape[0]
    u = ac_ref[...].astype(F32) * ax_ref[...].astype(F32)

    @pl.when(t == 0)
    def _():
        carry_ref[...] = buf_ref[0]

    prev = carry_ref[...]
    row = lax.broadcasted_iota(jnp.int32, u.shape, 0)
    u1 = jnp.where(row == 0, prev[1:2], pltpu.roll(u, 1, 0))
    u2 = jnp.where(row == 0, prev[0:1], jnp.where(row == 1, prev[1:2], pltpu.roll(u, 2, 0)))
    w = w_ref[...]
    conv = w[0:1] * u2 + w[1:2] * u1 + w[2:3] * u
    y_ref[...] = (ab_ref[...].astype(F32) * conv).astype(y_ref.dtype)
    carry_ref[...] = u[rows - 2:rows]

    @pl.when(t == nt - 1)
    def _():
        new_ref[0] = u[last_valid - 2:last_valid]


def short_conv(z, conv_buf, conv_w, *, batch, rows_per_seq, rows_blk, valid_rows, col_blocks):
    width = conv_w.shape[1]
    rows_blk = min(rows_blk, rows_per_seq)
    nt = rows_per_seq // rows_blk
    last_valid = valid_rows - (nt - 1) * rows_blk
    assert last_valid >= 2
    cx, cb, cc = col_blocks

    def zspec(cblk):
        return pl.BlockSpec((rows_blk, width), lambda b, t, cblk=cblk: (b * nt + t, cblk))

    return pl.pallas_call(
        functools.partial(_conv_kernel, last_valid=last_valid),
        grid=(batch, nt),
        in_specs=[zspec(cx), zspec(cb), zspec(cc),
                  pl.BlockSpec((1, 2, width), lambda b, t: (b, 0, 0)),
                  pl.BlockSpec((3, width), lambda b, t: (0, 0))],
        out_specs=[pl.BlockSpec((rows_blk, width), lambda b, t: (b * nt + t, 0)),
                   pl.BlockSpec((1, 2, width), lambda b, t: (b, 0, 0))],
        out_shape=[jax.ShapeDtypeStruct((batch * rows_per_seq, width), BF16),
                   jax.ShapeDtypeStruct((batch, 2, width), F32)],
        scratch_shapes=[pltpu.VMEM((2, width), F32)],
        compiler_params=_params(("parallel", "arbitrary"), rows_blk * width * 40),
        name="short_conv",
    )(z, z, z, conv_buf, conv_w)


def _cumsum_rows(x):
    rows = x.shape[0]
    row = lax.broadcasted_iota(jnp.int32, x.shape, 0)
    shift = 1
    while shift < rows:
        x = x + jnp.where(row >= shift, pltpu.roll(x, shift, 0), 0.0)
        shift *= 2
    return x


def _log_sigmoid(x):
    return jnp.minimum(x, 0.0) - jnp.log1p(jnp.exp(-jnp.abs(x)))


def _mlstm_kernel(q_ref, k_ref, v_ref, og_ref, g_ref, gbias_ref, gain_ref, c0_ref, n0_ref, m0_ref,
                  y_ref, c_ref, n_ref, m_ref, *, valid, heads, dk, dv):
    chunk = pl.program_id(1)
    L = CHUNK_ROWS
    lb = q_ref.shape[0]

    @pl.when(chunk == 0)
    def _():
        c_ref[...] = c0_ref[...]
        n_ref[...] = n0_ref[...]
        m_ref[...] = m0_ref[...]

    row = lax.broadcasted_iota(jnp.int32, (L, 1), 0)
    tri = lax.broadcasted_iota(jnp.int32, (L, L), 1) <= lax.broadcasted_iota(jnp.int32, (L, L), 0)
    row_ok = row < valid
    gates = _pad_rows(g_ref[...], L) + gbias_ref[...]
    ig_all = jnp.where(row_ok, gates, NEG)
    lf_all = jnp.where(row_ok, _log_sigmoid(gates), 0.0)
    b_all = _cumsum_rows(lf_all)
    ig_all_t = ig_all.T
    b_all_t = b_all.T
    scale = dk ** -0.5
    ys = []
    for h in range(heads):
        q = _pad_rows(q_ref[:, h * dk:(h + 1) * dk], L)
        k = _pad_rows(k_ref[:, h * dk:(h + 1) * dk], L)
        v = _pad_rows(v_ref[:, h * dv:(h + 1) * dv], L)
        ig_c = ig_all[:, h:h + 1]
        b_c = b_all[:, heads + h:heads + h + 1]
        ig_r = ig_all_t[h:h + 1, :]
        b_r = b_all_t[heads + h:heads + h + 1, :]
        c0 = c_ref[0, h]
        n0 = n_ref[0, h:h + 1, :]
        m0 = m_ref[0, :, h:h + 1]

        dmat = jnp.where(tri, b_c + (ig_r - b_r), NEG)
        inter = b_c + m0
        m_c = jnp.maximum(inter, jnp.max(dmat, axis=-1, keepdims=True))
        w_inter = jnp.exp(inter - m_c)
        qk = _nt_dot(q, k) * scale
        w_intra = jnp.exp(dmat - m_c) * qk
        num = w_inter * jnp.dot(q, c0.astype(BF16), preferred_element_type=F32)
        num = num + jnp.dot(w_intra.astype(BF16), v, preferred_element_type=F32)
        qn0 = jnp.sum(q.astype(F32) * n0, axis=-1, keepdims=True)
        den = w_inter * qn0 + jnp.sum(w_intra, axis=-1, keepdims=True)
        hid = num / jnp.maximum(jnp.abs(den), jnp.exp(-m_c))

        b_end = b_c[valid - 1:valid, :]
        m_end = m_c[valid - 1:valid, :]
        g_state = jnp.exp(b_end + m0 - m_end)
        g_tok = jnp.exp(b_end - b_c + ig_c - m_end) * scale
        kg = k.astype(F32) * g_tok
        c_ref[0, h] = g_state * c0 + jnp.dot(kg.T.astype(BF16), v, preferred_element_type=F32)
        n_ref[0, h:h + 1, :] = g_state * n0 + jnp.sum(kg, axis=0, keepdims=True)
        m_ref[0, :, h:h + 1] = m_end

        hn = _group_rms(hid, gain_ref[:, h * dv:(h + 1) * dv], dv)
        og = _pad_rows(og_ref[:, h * dv:(h + 1) * dv], L).astype(F32)
        ys.append(hn * jax.nn.sigmoid(og))
    y = jnp.concatenate(ys, axis=1)
    y_ref[...] = y[:lb].astype(y_ref.dtype)


def mlstm_mixer(z, gates, gate_bias, gain, c0, n0, m0, *, batch, rows_per_seq, valid_rows, cols):
    _, heads, dk, dv = c0.shape
    lb = min(CHUNK_ROWS, rows_per_seq)
    nc = rows_per_seq // lb
    assert rows_per_seq % lb == 0
    valid = valid_rows - (nc - 1) * lb
    assert (nc == 1 or valid == lb) and 1 <= valid <= lb
    q_off, k_off, v_off, o_off = cols
    wq, wv = heads * dk, heads * dv

    def zspec(width, off):
        assert off % width == 0
        return pl.BlockSpec((lb, width), lambda b, c, blk=off // width: (b * nc + c, blk))

    m0 = m0.reshape(batch, 1, heads)
    y, c, n, m = pl.pallas_call(
        functools.partial(_mlstm_kernel, valid=valid, heads=heads, dk=dk, dv=dv),
        grid=(batch, nc),
        in_specs=[zspec(wq, q_off), zspec(wq, k_off), zspec(wv, v_off), zspec(wv, o_off),
                  pl.BlockSpec((lb, V7X_LANES), lambda b, c: (b * nc + c, 0)),
                  pl.BlockSpec((1, V7X_LANES), lambda b, c: (0, 0)),
                  pl.BlockSpec((1, wv), lambda b, c: (0, 0)),
                  pl.BlockSpec((1, heads, dk, dv), lambda b, c: (b, 0, 0, 0)),
                  pl.BlockSpec((1, heads, dk), lambda b, c: (b, 0, 0)),
                  pl.BlockSpec((1, 1, heads), lambda b, c: (b, 0, 0))],
        out_specs=[pl.BlockSpec((lb, wv), lambda b, c: (b * nc + c, 0)),
                   pl.BlockSpec((1, heads, dk, dv), lambda b, c: (b, 0, 0, 0)),
                   pl.BlockSpec((1, heads, dk), lambda b, c: (b, 0, 0)),
                   pl.BlockSpec((1, 1, heads), lambda b, c: (b, 0, 0))],
        out_shape=[jax.ShapeDtypeStruct((batch * rows_per_seq, wv), BF16),
                   jax.ShapeDtypeStruct(c0.shape, F32),
                   jax.ShapeDtypeStruct(n0.shape, F32),
                   jax.ShapeDtypeStruct((batch, 1, heads), F32)],
        compiler_params=_params(("parallel", "arbitrary"), 16 << 20),
        name="mlstm",
    )(z, z, z, z, gates, gate_bias, gain.reshape(1, wv), c0, n0, m0)
    return y, c, n, m.reshape(batch, heads)


def _retention_kernel(q_ref, k_ref, v_ref, gd_ref, gain_ref, s0_ref, y_ref, s_ref, *, valid, heads, dk, dv):
    chunk = pl.program_id(1)
    L = CHUNK_ROWS
    lb = q_ref.shape[0]

    @pl.when(chunk == 0)
    def _():
        s_ref[...] = s0_ref[...]

    row = lax.broadcasted_iota(jnp.int32, (L, 1), 0)
    t_idx = lax.broadcasted_iota(jnp.int32, (L, L), 0)
    s_idx = lax.broadcasted_iota(jnp.int32, (L, L), 1)
    dist = (t_idx - s_idx).astype(F32)
    keep = (t_idx >= s_idx) & (s_idx < valid)
    rowf = row.astype(F32)
    scale = dk ** -0.5
    ys = []
    for h in range(heads):
        log_gamma = math.log1p(-(2.0 ** (-5.0 - h)))
        q = _pad_rows(q_ref[:, h * dk:(h + 1) * dk], L)
        k = _pad_rows(k_ref[:, h * dk:(h + 1) * dk], L)
        v = _pad_rows(v_ref[:, h * dv:(h + 1) * dv], L)
        s0 = s_ref[0, h]
        decay = jnp.where(keep, jnp.exp(jnp.maximum(dist, 0.0) * log_gamma), 0.0)
        smat = _nt_dot(q, k) * scale * decay
        inter = jnp.exp((rowf + 1.0) * log_gamma)
        o = jnp.dot(smat.astype(BF16), v, preferred_element_type=F32)
        o = o + inter * jnp.dot(q, s0.astype(BF16), preferred_element_type=F32)
        tail = jnp.where(row < valid, jnp.exp((valid - 1.0 - rowf) * log_gamma), 0.0) * scale
        kt = k.astype(F32) * tail
        s_ref[0, h] = math.exp(valid * log_gamma) * s0 + jnp.dot(kt.T.astype(BF16), v, preferred_element_type=F32)
        on = _group_rms(o, gain_ref[:, h * dv:(h + 1) * dv], dv)
        gd = _pad_rows(gd_ref[:, h * dv:(h + 1) * dv], L).astype(F32)
        ys.append(on * (gd * jax.nn.sigmoid(gd)))
    y = jnp.concatenate(ys, axis=1)
    y_ref[...] = y[:lb].astype(y_ref.dtype)


def retention_mixer(z, gain, s0, *, batch, rows_per_seq, valid_rows, cols):
    _, heads, dk, dv = s0.shape
    lb = min(CHUNK_ROWS, rows_per_seq)
    nc = rows_per_seq // lb
    valid = valid_rows - (nc - 1) * lb
    assert (nc == 1 or valid == lb) and 1 <= valid <= lb
    q_off, k_off, v_off, g_off = cols
    wq, wv = heads * dk, heads * dv

    def zspec(width, off):
        assert off % width == 0
        return pl.BlockSpec((lb, width), lambda b, c, blk=off // width: (b * nc + c, blk))

    return pl.pallas_call(
        functools.partial(_retention_kernel, valid=valid, heads=heads, dk=dk, dv=dv),
        grid=(batch, nc),
        in_specs=[zspec(wq, q_off), zspec(wq, k_off), zspec(wv, v_off), zspec(wv, g_off),
                  pl.BlockSpec((1, wv), lambda b, c: (0, 0)),
                  pl.BlockSpec((1, heads, dk, dv), lambda b, c: (b, 0, 0, 0))],
        out_specs=[pl.BlockSpec((lb, wv), lambda b, c: (b * nc + c, 0)),
                   pl.BlockSpec((1, heads, dk, dv), lambda b, c: (b, 0, 0, 0))],
        out_shape=[jax.ShapeDtypeStruct((batch * rows_per_seq, wv), BF16),
                   jax.ShapeDtypeStruct(s0.shape, F32)],
        compiler_params=_params(("parallel", "arbitrary"), 16 << 20),
        name="retention",
    )(z, z, z, z, gain.reshape(1, wv), s0)


def _store_cache_rows(ref, x, heads):
    rows, width = x.shape
    dv = width // heads
    tiles_per_head = dv // V7X_LANES
    step = heads * tiles_per_head
    for h in range(heads):
        for t in range(tiles_per_head):
            c0 = h * dv + t * V7X_LANES
            ref[pl.ds(t * heads + h, rows, stride=step), :] = x[:, c0:c0 + V7X_LANES]


def _load_cache_head(ref3, h, heads, tokens, dv):
    tiles_per_head = dv // V7X_LANES
    step = heads * tiles_per_head
    return jnp.concatenate([ref3[0, pl.ds(t * heads + h, tokens, stride=step), :]
                            for t in range(tiles_per_head)], axis=1)


def _cache_view(a, lead):
    tokens, heads, dv = a.shape[-3:]
    t = dv // V7X_LANES
    return (a.reshape(lead, tokens, heads, t, V7X_LANES).transpose(0, 1, 3, 2, 4)
            .reshape(lead, tokens * t * heads, V7X_LANES))


def _cache_unview(a, batch, tokens, heads, dv):
    t = dv // V7X_LANES
    return (a.reshape(batch, tokens, t, heads, V7X_LANES).transpose(0, 1, 3, 2, 4)
            .reshape(batch, tokens, heads, dv))


def _diff_prep_kernel(q_ref, k_ref, v_ref, gq_ref, gk_ref, qn_ref, kb_ref, kc_ref, vc_ref, *, dh, heads):
    qn_ref[...] = (_group_rms(q_ref[...].astype(F32), gq_ref[...], dh) * dh ** -0.5).astype(qn_ref.dtype)
    kn = _group_rms(k_ref[...].astype(F32), gk_ref[...], dh)
    kb_ref[...] = kn.astype(kb_ref.dtype)
    _store_cache_rows(kc_ref, kn, heads)
    _store_cache_rows(vc_ref, v_ref[...].astype(F32), heads)


def diff_prep(z, g_q, g_k, *, rows_blk, cols, width, heads):
    m = z.shape[0]
    dh = g_q.shape[0]
    rows_blk = min(rows_blk, m)
    q_off, k_off, v_off = cols
    il = width // V7X_LANES

    def zspec(off):
        assert off % width == 0
        return pl.BlockSpec((rows_blk, width), lambda i, blk=off // width: (i, blk))

    ospec = pl.BlockSpec((rows_blk, width), lambda i: (i, 0))
    cspec = pl.BlockSpec((rows_blk * il, V7X_LANES), lambda i: (i, 0))
    return pl.pallas_call(
        functools.partial(_diff_prep_kernel, dh=dh, heads=heads),
        grid=(m // rows_blk,),
        in_specs=[zspec(q_off), zspec(k_off), zspec(v_off),
                  pl.BlockSpec((1, dh), lambda i: (0, 0)), pl.BlockSpec((1, dh), lambda i: (0, 0))],
        out_specs=[ospec, ospec, cspec, cspec],
        out_shape=[jax.ShapeDtypeStruct((m, width), BF16),
                   jax.ShapeDtypeStruct((m, width), BF16),
                   jax.ShapeDtypeStruct((m * il, V7X_LANES), F32),
                   jax.ShapeDtypeStruct((m * il, V7X_LANES), F32)],
        compiler_params=_params(("parallel",), rows_blk * width * 48),
        name="diff_prep",
    )(z, z, z, g_q.reshape(1, dh), g_k.reshape(1, dh))


def _softmax_update(s, v, m_ref, l_ref, acc_ref):
    m_old = m_ref[...]
    m_new = jnp.maximum(m_old, jnp.max(s, axis=-1, keepdims=True))
    alpha = jnp.exp(m_old - m_new)
    p = jnp.exp(s - m_new)
    l_ref[...] = alpha * l_ref[...] + jnp.sum(p, axis=-1, keepdims=True)
    acc_ref[...] = alpha * acc_ref[...] + jnp.dot(p.astype(BF16), v, preferred_element_type=F32)
    m_ref[...] = m_new


def _diff_attn_prompt_kernel(slope_ref, lam_ref, q_ref, k_ref, v_ref, gain_ref, y_ref, m_s, l_s, acc_s,
                             *, bq, bk, dh, out_scale, rows_sub):
    h = pl.program_id(1)
    qi = pl.program_id(2)
    ki = pl.program_id(3)

    @pl.when(ki == 0)
    def _():
        m_s[...] = jnp.full(m_s.shape, NEG, F32)
        l_s[...] = jnp.zeros(l_s.shape, F32)
        acc_s[...] = jnp.zeros(acc_s.shape, F32)

    def block_update(diagonal):
        k = k_ref[...]
        v = v_ref[...]
        rel = lax.broadcasted_iota(jnp.int32, (1, bk), 1) + (ki * bk - qi * bq)
        bias = slope_ref[h] * rel.astype(F32)
        for r in range(bq // rows_sub):
            rows = pl.ds(r * rows_sub, rows_sub)
            if diagonal:
                qrow = lax.broadcasted_iota(jnp.int32, (rows_sub, 1), 0) + r * rows_sub
                row_bias = jnp.where(rel <= qrow, bias, NEG)
            else:
                row_bias = bias
            for j in range(2):
                s = _nt_dot(q_ref[rows, j * dh:(j + 1) * dh], k[:, j * dh:(j + 1) * dh]) + row_bias
                _softmax_update(s, v, m_s.at[j, rows], l_s.at[j, rows], acc_s.at[j, rows])

    @pl.when(ki < qi)
    def _():
        block_update(False)

    @pl.when(ki == qi)
    def _():
        block_update(True)
        o = acc_s[0] / l_s[0] - lam_ref[0] * (acc_s[1] / l_s[1])
        y_ref[...] = (_group_rms(o, gain_ref[...], o.shape[1]) * out_scale).astype(y_ref.dtype)


def diff_attention_prompt(qn, kn, z, slopes, lam, gain, *, batch, seq, heads, dh, dv, v_off, blk, out_scale):
    bq = bk = min(blk, seq)
    nq = seq // bq
    hw = 2 * dh
    assert hw == dv and v_off % dv == 0
    vblk = v_off // dv

    def kv_row(b, qi, ki):
        return b * nq + jnp.minimum(ki, qi)

    grid_spec = pltpu.PrefetchScalarGridSpec(
        num_scalar_prefetch=0,
        grid=(batch, heads, nq, nq),
        in_specs=[pl.BlockSpec(memory_space=pltpu.SMEM),
                  pl.BlockSpec(memory_space=pltpu.SMEM),
                  pl.BlockSpec((bq, hw), lambda b, h, qi, ki: (b * nq + qi, h)),
                  pl.BlockSpec((bk, hw), lambda b, h, qi, ki: (kv_row(b, qi, ki), h)),
                  pl.BlockSpec((bk, dv), lambda b, h, qi, ki: (kv_row(b, qi, ki), vblk + h)),
                  pl.BlockSpec((1, dv), lambda b, h, qi, ki: (0, h))],
        out_specs=pl.BlockSpec((bq, dv), lambda b, h, qi, ki: (b * nq + qi, h)),
        scratch_shapes=[pltpu.VMEM((2, bq, 1), F32), pltpu.VMEM((2, bq, 1), F32), pltpu.VMEM((2, bq, dv), F32)],
    )
    return pl.pallas_call(
        functools.partial(_diff_attn_prompt_kernel, bq=bq, bk=bk, dh=dh, out_scale=out_scale,
                          rows_sub=min(256, bq)),
        grid_spec=grid_spec,
        out_shape=jax.ShapeDtypeStruct((batch * seq, heads * dv), BF16),
        compiler_params=_params(("parallel", "parallel", "parallel", "arbitrary"), 24 << 20),
        name="diff_attn_prompt",
    )(slopes, lam, qn, kn, z, gain.reshape(1, heads * dv))


def _diff_attn_decode_kernel(pt_ref, slope_ref, lam_ref, q_ref, kn_ref, vn_ref, gain_ref, *rest,
                             pages, page, past, valid, heads, dh, dv, out_scale):
    k_refs = rest[:pages]
    v_refs = rest[pages:2 * pages]
    y_ref, m_s, l_s, acc_s = rest[2 * pages:]
    step = pl.program_id(1)
    nsteps = pl.num_programs(1)
    rows = q_ref.shape[0]
    hw = 2 * dh

    @pl.when(step == 0)
    def _():
        m_s[...] = jnp.full(m_s.shape, NEG, F32)
        l_s[...] = jnp.zeros(l_s.shape, F32)
        acc_s[...] = jnp.zeros(acc_s.shape, F32)

    def stacked_queries(h):
        qh = q_ref[:, h * hw:(h + 1) * hw]
        zero = jnp.zeros((rows, dh), qh.dtype)
        return jnp.concatenate([jnp.concatenate([qh[:, :dh], zero], axis=1),
                                jnp.concatenate([zero, qh[:, dh:]], axis=1)], axis=0)

    nkeys = pages * page
    rel = lax.broadcasted_iota(jnp.int32, (1, nkeys), 1) + (step * nkeys - past)
    for h in range(heads):
        qs = stacked_queries(h)
        k = jnp.concatenate([_load_cache_head(r, h, heads, page, hw) for r in k_refs], axis=0).astype(BF16)
        v = jnp.concatenate([_load_cache_head(r, h, heads, page, dv) for r in v_refs], axis=0).astype(BF16)
        s = _nt_dot(qs, k) + slope_ref[h] * rel.astype(F32)
        _softmax_update(s, v, m_s.at[h], l_s.at[h], acc_s.at[h])

    @pl.when(step == nsteps - 1)
    def _():
        nk = V7X_LANES
        key = lax.broadcasted_iota(jnp.int32, (1, nk), 1)
        tok = lax.broadcasted_iota(jnp.int32, (2 * rows, 1), 0) & (rows - 1)
        ok = (key <= tok) & (key < valid)
        for h in range(heads):
            qs = stacked_queries(h)
            k = _pad_rows(kn_ref[:, h * hw:(h + 1) * hw], nk)
            v = _pad_rows(vn_ref[:, h * dv:(h + 1) * dv], nk)
            s = _nt_dot(qs, k) + slope_ref[h] * key.astype(F32)
            s = jnp.where(ok, s, NEG)
            _softmax_update(s, v, m_s.at[h], l_s.at[h], acc_s.at[h])
            o = acc_s[h] / l_s[h]
            o = o[:rows] - lam_ref[0] * o[rows:]
            y_ref[:, h * dv:(h + 1) * dv] = (
                _group_rms(o, gain_ref[:, h * dv:(h + 1) * dv], dv) * out_scale).astype(y_ref.dtype)


def diff_attention_decode(qn, kn, z, cache_k, cache_v, page_table, layer, slopes, lam, gain,
                          *, batch, rows_per_seq, valid_rows, heads, dh, dv, v_off, out_scale):
    depth, n_pool, page = cache_k.shape[:3]
    n_pages = page_table.shape[1]
    pages = PAGES_PER_STEP
    assert n_pages % pages == 0
    nsteps = n_pages // pages
    width = heads * dv
    ck = _cache_view(cache_k, depth * n_pool)
    cv = _cache_view(cache_v, depth * n_pool)
    base = layer * n_pool
    page_rows = page * width // V7X_LANES

    def page_spec(p):
        return pl.BlockSpec((1, page_rows, V7X_LANES),
                            lambda b, s, pt, p=p: (base + pt[b * n_pages + s * pages + p], 0, 0))

    row_spec = pl.BlockSpec((rows_per_seq, width), lambda b, s, pt: (b, 0))
    grid_spec = pltpu.PrefetchScalarGridSpec(
        num_scalar_prefetch=1,
        grid=(batch, nsteps),
        in_specs=[pl.BlockSpec(memory_space=pltpu.SMEM),
                  pl.BlockSpec(memory_space=pltpu.SMEM),
                  row_spec, row_spec,
                  pl.BlockSpec((rows_per_seq, width), lambda b, s, pt: (b, v_off // width)),
                  pl.BlockSpec((1, width), lambda b, s, pt: (0, 0))]
                 + [page_spec(p) for p in range(pages)] + [page_spec(p) for p in range(pages)],
        out_specs=row_spec,
        scratch_shapes=[pltpu.VMEM((heads, 2 * rows_per_seq, 1), F32),
                        pltpu.VMEM((heads, 2 * rows_per_seq, 1), F32),
                        pltpu.VMEM((heads, 2 * rows_per_seq, dv), F32)],
    )
    assert v_off % width == 0
    return pl.pallas_call(
        functools.partial(_diff_attn_decode_kernel, pages=pages, page=page, past=n_pages * page,
                          valid=valid_rows, heads=heads, dh=dh, dv=dv, out_scale=out_scale),
        grid_spec=grid_spec,
        out_shape=jax.ShapeDtypeStruct((batch * rows_per_seq, width), BF16),
        compiler_params=_params(("parallel", "arbitrary"), 4 * pages * page * width * 4 + (8 << 20)),
        name="diff_attn_decode",
    )(page_table.reshape(-1), slopes, lam, qn, kn, z, gain.reshape(1, width),
      *([ck] * pages), *([cv] * pages))


def _cross_attn_kernel(q_ref, mk_ref, mv_ref, o_ref, *, heads, dh, n_mem, cache_layout):
    outs = []
    for h in range(heads):
        q = q_ref[:, h * dh:(h + 1) * dh]
        if cache_layout:
            mk = _load_cache_head(mk_ref, h, heads, n_mem, dh).astype(BF16)
            mv = _load_cache_head(mv_ref, h, heads, n_mem, dh).astype(BF16)
        else:
            mk = mk_ref[:, h * dh:(h + 1) * dh].astype(BF16)
            mv = mv_ref[:, h * dh:(h + 1) * dh].astype(BF16)
        s = _nt_dot(q, mk) * dh ** -0.5
        p = jnp.exp(s - jnp.max(s, axis=-1, keepdims=True))
        o = jnp.dot(p.astype(BF16), mv, preferred_element_type=F32)
        outs.append(o / jnp.sum(p, axis=-1, keepdims=True))
    o_ref[...] = jnp.concatenate(outs, axis=1).astype(o_ref.dtype)


def cross_attention(qx, mk, mv, *, batch, rows_per_seq, n_mem, heads, dh, rows_blk, cache_layer=None):
    width = heads * dh
    rows_blk = min(rows_blk, rows_per_seq)
    nq = rows_per_seq // rows_blk
    if cache_layer is None:
        mem_spec = pl.BlockSpec((n_mem, width), lambda b, i: (b, 0))
    else:
        lead = mk.shape[0] * mk.shape[1]
        mk, mv = _cache_view(mk, lead), _cache_view(mv, lead)
        mem_spec = pl.BlockSpec((1, n_mem * width // V7X_LANES, V7X_LANES),
                                lambda b, i: (cache_layer * batch + b, 0, 0))
    return pl.pallas_call(
        functools.partial(_cross_attn_kernel, heads=heads, dh=dh, n_mem=n_mem,
                          cache_layout=cache_layer is not None),
        grid=(batch, nq),
        in_specs=[pl.BlockSpec((rows_blk, width), lambda b, i: (b * nq + i, 0)), mem_spec, mem_spec],
        out_specs=pl.BlockSpec((rows_blk, width), lambda b, i: (b * nq + i, 0)),
        out_shape=jax.ShapeDtypeStruct((batch * rows_per_seq, width), BF16),
        compiler_params=_params(("parallel", "parallel"), 24 << 20),
        name="cross_attn",
    )(qx, mk, mv)


def kernel(x_prompt, x_sample, cache_diff_k, cache_diff_v, cache_mem_k, cache_mem_v, state_conv, state_mlstm_C, state_mlstm_n, state_mlstm_m, state_ret, page_table, mem_prompt, norm_mix, w_in, conv_w, b_igate, b_fgate, g_mlstm, g_diff_q, g_diff_k, lam_q1, lam_k1, lam_q2, lam_k2, g_diff_out, g_ret, w_out, norm_xattn, norm_mem, w_xq, w_xk, w_xv, g_xq, g_xk, w_xo, norm_mlp, w_up, w_down):
    depth = w_in.shape[0]
    bp, seq, d_model = x_prompt.shape
    bs, dec = x_sample.shape[:2]
    conv_ch = conv_w.shape[2]
    _, _, hb, dkb, dvb = state_mlstm_C.shape
    _, _, hd, dkd, dvd = state_ret.shape
    hc, dvc = cache_diff_v.shape[3:]
    dhc = dvc // 2
    _, _, n_mem, hx, dhx = cache_mem_k.shape
    xw = hx * dhx
    assert seq % CHUNK_ROWS == 0 and dec <= SAMPLE_ROWS

    sizes = (conv_ch, conv_ch, conv_ch, hb * dkb, hb * dkb, hb * dvb, hb * dvb, hb, hb,
             2 * hc * dhc, 2 * hc * dhc, hc * dvc, hd * dkd, hd * dkd, hd * dvd, hd * dvd)
    names = ("a_x", "a_b", "a_c", "q_b", "k_b", "v_b", "o_b", "i_b", "f_b",
             "q_c", "k_c", "v_c", "q_d", "k_d", "v_d", "g_d")
    off, pos = {}, 0
    for nm, sz in zip(names, sizes):
        if nm not in ("i_b", "f_b"):
            off[nm] = pos
            pos += sz

    slopes = (2.0 ** (-8.0 * jnp.arange(1, hc + 1, dtype=F32) / hc)).astype(F32)

    x_p = x_prompt.reshape(bp * seq, d_model)
    x_s = jnp.pad(x_sample, ((0, 0), (0, SAMPLE_ROWS - dec), (0, 0))).reshape(bs * SAMPLE_ROWS, d_model)
    mem_rows = mem_prompt.reshape(bp * n_mem, d_model)

    groups = (
        dict(batch=bp, rows=seq, valid=seq, tm=1024, prompt=True),
        dict(batch=bs, rows=SAMPLE_ROWS, valid=dec, tm=bs * SAMPLE_ROWS, prompt=False),
    )
    xs = [x_p, x_s]
    outs = [dict(conv=[], C=[], n=[], m=[], k=[], v=[], S=[], mk=[], mv=[]) for _ in groups]

    gate_lo = sum(sizes[:7])
    w_main, w_gate = split_w_in(w_in, gate_lo, gate_lo + 2 * hb)
    w_out_b, w_xq_b, w_xk_b, w_xv_b, w_xo_b = (cast_bf16(w) for w in (w_out, w_xq, w_xk, w_xv, w_xo))

    for li in range(depth):
        lam_init = 0.8 - 0.6 * math.exp(-0.3 * li)
        lam = (jnp.exp(jnp.sum(lam_q1[li] * lam_k1[li])) - jnp.exp(jnp.sum(lam_q2[li] * lam_k2[li]))
               + lam_init).reshape(1).astype(F32)
        gate_bias = jnp.pad(jnp.concatenate([b_igate[li], b_fgate[li]]), (0, V7X_LANES - 2 * hb)).reshape(1, V7X_LANES)

        hm = rmsnorm_rows(mem_rows, norm_mem[li], 256)
        mk_p = matmul([hm], w_xk_b, li, tm=1024, tn=1024, out_dtype=F32,
                      epilogue="headnorm", extra=g_xk[li], group=dhx, name="mm_mem_k")
        mv_p = matmul([hm], w_xv_b, li, tm=1024, tn=1024, out_dtype=F32, name="mm_mem_v")

        for gi, g in enumerate(groups):
            x = xs[gi]
            batch, rows, valid, tm = g["batch"], g["rows"], g["valid"], g["tm"]
            if g["prompt"]:
                conv_buf = jnp.zeros((batch, 2, conv_ch), F32)
                c0 = jnp.zeros((batch, hb, dkb, dvb), F32)
                n0 = jnp.zeros((batch, hb, dkb), F32)
                m0 = jnp.zeros((batch, hb), F32)
                s0 = jnp.zeros((batch, hd, dkd, dvd), F32)
                mk, mv, cache_layer = mk_p, mv_p, None
            else:
                conv_buf, c0, n0, m0, s0 = (state_conv[li], state_mlstm_C[li], state_mlstm_n[li],
                                            state_mlstm_m[li], state_ret[li])
                mk, mv, cache_layer = cache_mem_k, cache_mem_v, li

            h = rmsnorm_rows(x, norm_mix[li], 256)
            if g["prompt"]:
                z, w_up_l = matmul([h], w_main, li, tm=tm, tn=1024, side=(w_up, li), name="mm_in")
            else:
                z = matmul([h], w_main, li, tm=tm, tn=1024, name="mm_in")
            gates = matmul([h], w_gate, li, tm=tm, tn=V7X_LANES, out_dtype=F32, name="mm_gate")
            y_a, conv_new = short_conv(z, conv_buf, conv_w[li], batch=batch, rows_per_seq=rows, rows_blk=512,
                                       valid_rows=valid,
                                       col_blocks=(off["a_x"] // conv_ch, off["a_b"] // conv_ch, off["a_c"] // conv_ch))
            y_b, c_new, n_new, m_new = mlstm_mixer(
                z, gates, gate_bias, g_mlstm[li], c0, n0, m0, batch=batch, rows_per_seq=rows, valid_rows=valid,
                cols=(off["q_b"], off["k_b"], off["v_b"], off["o_b"]))
            qn, kb, kc, vc = diff_prep(z, g_diff_q[li], g_diff_k[li], rows_blk=512,
                                       cols=(off["q_c"], off["k_c"], off["v_c"]), width=hc * dvc, heads=hc)
            if g["prompt"]:
                y_c = diff_attention_prompt(qn, kb, z, slopes, lam, g_diff_out[li], batch=batch, seq=rows,
                                            heads=hc, dh=dhc, dv=dvc, v_off=off["v_c"], blk=512,
                                            out_scale=1.0 - lam_init)
            else:
                y_c = diff_attention_decode(qn, kb, z, cache_diff_k, cache_diff_v, page_table, li, slopes, lam,
                                            g_diff_out[li], batch=batch, rows_per_seq=rows, valid_rows=valid,
                                            heads=hc, dh=dhc, dv=dvc, v_off=off["v_c"], out_scale=1.0 - lam_init)
            y_d, s_new = retention_mixer(z, g_ret[li], s0, batch=batch, rows_per_seq=rows, valid_rows=valid,
                                         cols=(off["q_d"], off["k_d"], off["v_d"], off["g_d"]))
            x = matmul([y_a, y_b, y_c, y_d], w_out_b, li, tm=tm, tn=512, out_dtype=F32,
                       epilogue="residual", extra=x, name="mm_out")

            h2 = rmsnorm_rows(x, norm_xattn[li], 256)
            qx = matmul([h2], w_xq_b, li, tm=tm, tn=1024, epilogue="headnorm", extra=g_xq[li], group=dhx,
                        name="mm_xq")
            ox = cross_attention(qx, mk, mv, batch=batch, rows_per_seq=rows, n_mem=n_mem, heads=hx, dh=dhx,
                                 rows_blk=512, cache_layer=cache_layer)
            x = matmul([ox], w_xo_b, li, tm=tm, tn=1024, out_dtype=F32, epilogue="residual", extra=x, name="mm_xo")

            h3 = rmsnorm_rows(x, norm_mlp[li], 256)
            if g["prompt"]:
                hid, w_down_l = matmul([h3], w_up_l, 0, tm=tm, tn=1024, epilogue="relu2",
                                       side=(w_down, li), name="mm_up")
            else:
                hid = matmul([h3], w_up_l, 0, tm=tm, tn=1024, epilogue="relu2", name="mm_up")
            x = matmul([hid], w_down_l, 0, tm=tm, tn=1024, tk=4096, out_dtype=F32,
                       epilogue="residual", extra=x, name="mm_down")

            xs[gi] = x
            o = outs[gi]
            o["conv"].append(conv_new); o["C"].append(c_new); o["n"].append(n_new); o["m"].append(m_new)
            o["k"].append(kc); o["v"].append(vc); o["S"].append(s_new)
            if g["prompt"]:
                o["mk"].append(mk_p); o["mv"].append(mv_p)

    op, os_ = outs
    y_prompt = xs[0].reshape(bp, seq, d_model)
    y_sample = xs[1].reshape(bs, SAMPLE_ROWS, d_model)[:, :dec]

    def cache_rows(parts, batch, rows, valid):
        a = jnp.stack([_cache_unview(p, batch, rows, hc, dvc) for p in parts])
        return a if rows == valid else a[:, :, :valid]

    return (y_prompt, y_sample,
            jnp.stack(op["conv"]), jnp.stack(op["C"]), jnp.stack(op["n"]), jnp.stack(op["m"]),
            cache_rows(op["k"], bp, seq, seq), cache_rows(op["v"], bp, seq, seq), jnp.stack(op["S"]),
            jnp.stack(op["mk"]).reshape(depth, bp, n_mem, hx, dhx),
            jnp.stack(op["mv"]).reshape(depth, bp, n_mem, hx, dhx),
            jnp.stack(os_["conv"]), jnp.stack(os_["C"]), jnp.stack(os_["n"]), jnp.stack(os_["m"]),
            cache_rows(os_["k"], bs, SAMPLE_ROWS, dec), cache_rows(os_["v"], bs, SAMPLE_ROWS, dec),
            jnp.stack(os_["S"]))
```

```python
import functools
import math

import jax
import jax.numpy as jnp
from jax import lax
from jax.experimental import pallas as pl
from jax.experimental.pallas import tpu as pltpu

V7X_LANES = 128
V7X_VMEM_BYTES = 64 * 1024 * 1024
VMEM_LIMIT_CAP = V7X_VMEM_BYTES - 6 * 1024 * 1024

EPS = 1e-6
NEG = -1e30
CHUNK_ROWS = 128
SAMPLE_ROWS = 16
PAGES_PER_STEP = 8

BF16 = jnp.bfloat16
F32 = jnp.float32


def _params(semantics, vmem_estimate):
    limit = int(min(max(vmem_estimate * 5 // 4 + (4 << 20), 32 << 20), VMEM_LIMIT_CAP))
    return pltpu.CompilerParams(dimension_semantics=semantics, vmem_limit_bytes=limit)


def _nt_dot(a, b):
    return lax.dot_general(a, b, (((1,), (1,)), ((), ())), preferred_element_type=F32)


def _pad_rows(x, rows):
    if x.shape[0] == rows:
        return x
    return jnp.concatenate([x, jnp.zeros((rows - x.shape[0],) + x.shape[1:], x.dtype)], axis=0)


def _group_rms(x, gain, width):
    parts = []
    for g in range(x.shape[1] // width):
        blk = x[:, g * width:(g + 1) * width]
        ms = jnp.mean(blk * blk, axis=-1, keepdims=True)
        parts.append(blk * lax.rsqrt(ms + EPS) * gain)
    return parts[0] if len(parts) == 1 else jnp.concatenate(parts, axis=1)


def _rmsnorm_kernel(x_ref, g_ref, o_ref):
    x = x_ref[...]
    ms = jnp.mean(x * x, axis=-1, keepdims=True)
    o_ref[...] = (x * lax.rsqrt(ms + EPS) * g_ref[...]).astype(o_ref.dtype)


def rmsnorm_rows(x, g, rows_blk):
    m, d = x.shape
    rows_blk = min(rows_blk, m)
    return pl.pallas_call(
        _rmsnorm_kernel,
        grid=(m // rows_blk,),
        in_specs=[pl.BlockSpec((rows_blk, d), lambda i: (i, 0)),
                  pl.BlockSpec((1, d), lambda i: (0, 0))],
        out_specs=pl.BlockSpec((rows_blk, d), lambda i: (i, 0)),
        out_shape=jax.ShapeDtypeStruct((m, d), BF16),
        compiler_params=_params(("parallel",), rows_blk * d * 12),
        name="rmsnorm",
    )(x, g.reshape(1, d))


def _cast_kernel(x_ref, o_ref):
    o_ref[...] = x_ref[...].astype(o_ref.dtype)


def cast_bf16(w, rows_blk=256, cols_blk=4096):
    d, k, n = w.shape
    tr, tc = min(rows_blk, k), min(cols_blk, n)
    assert k % tr == 0 and n % tc == 0
    spec = pl.BlockSpec((None, tr, tc), lambda l, i, j: (l, i, j))
    return pl.pallas_call(
        _cast_kernel,
        grid=(d, k // tr, n // tc),
        in_specs=[spec], out_specs=spec,
        out_shape=jax.ShapeDtypeStruct(w.shape, BF16),
        compiler_params=_params(("parallel", "parallel", "parallel"), tr * tc * 16),
        name="cast_bf16",
    )(w)


def _transpose_cast_kernel(x_ref, o_ref):
    o_ref[...] = x_ref[0].T.astype(o_ref.dtype)


def split_w_in(w_in, g0, g1, tn=1024, tk=512):
    d, k, n = w_in.shape
    glen = g1 - g0
    n_main = n - glen
    assert g0 % tn == 0 and n_main % tn == 0 and k % tk == 0 and glen % 8 == 0 and glen <= V7X_LANES
    w_t = jnp.swapaxes(w_in, 1, 2)

    def src_index(l, j, kk):
        row = j * tn + jnp.where(j * tn >= g0, glen, 0)
        return (l, pl.multiple_of(row, 8), pl.multiple_of(kk * tk, V7X_LANES))

    main = pl.pallas_call(
        _transpose_cast_kernel,
        grid=(d, n_main // tn, k // tk),
        in_specs=[pl.BlockSpec((pl.Element(1), pl.Element(tn), pl.Element(tk)), src_index)],
        out_specs=pl.BlockSpec((None, tk, tn), lambda l, j, kk: (l, kk, j)),
        out_shape=jax.ShapeDtypeStruct((d, k, n_main), BF16),
        compiler_params=_params(("parallel", "parallel", "parallel"), tn * tk * 24),
        name="split_w_in",
    )(w_t)
    gate = jnp.pad(w_in[:, :, g0:g1], ((0, 0), (0, 0), (0, V7X_LANES - glen))).astype(BF16)
    return main, gate


def _mm_kernel(*refs, n_a, kt, epilogue, group, side_blocks, steps_jk, row_scale_dim, norm_out):
    a_refs = refs[:n_a]
    w_refs = refs[n_a:2 * n_a]
    pos = 2 * n_a
    extra_ref = ssq_in = gain_next = side_in = None
    if epilogue in ("residual", "headnorm"):
        extra_ref = refs[pos]
        pos += 1
    if row_scale_dim:
        ssq_in = refs[pos]
        pos += 1
    if norm_out:
        gain_next = refs[pos]
        pos += 1
    if side_blocks:
        side_in = refs[pos]
        pos += 1
    o_ref = refs[pos]
    pos += 1
    if norm_out:
        xg_ref, ssq_out = refs[pos], refs[pos + 1]
        pos += 2
    side_out = refs[pos] if side_blocks else None

    if side_blocks:
        step = (pl.program_id(0) * steps_jk[0] + pl.program_id(1)) * steps_jk[1] + pl.program_id(2)

        @pl.when(step < side_blocks)
        def _():
            side_out[...] = side_in[...].astype(side_out.dtype)

    part = jnp.dot(a_refs[0][...], w_refs[0][...], preferred_element_type=F32)
    for a_ref, w_ref in zip(a_refs[1:], w_refs[1:]):
        part = part + jnp.dot(a_ref[...], w_ref[...], preferred_element_type=F32)

    if row_scale_dim:
        ssq = ssq_in[...]
        total = ssq[:, 0:1]
        for g in range(1, ssq.shape[1] // V7X_LANES):
            total = total + ssq[:, g * V7X_LANES:g * V7X_LANES + 1]
        part = part * lax.rsqrt(total * (1.0 / row_scale_dim) + EPS)

    def emit_norm_input(x_new):
        xg_ref[...] = (x_new * gain_next[...]).astype(xg_ref.dtype)
        ssq_out[...] = jnp.broadcast_to(jnp.sum(x_new * x_new, axis=-1, keepdims=True), ssq_out.shape)

    if kt > 1:
        k = pl.program_id(2)

        @pl.when(k == 0)
        def _():
            o_ref[...] = extra_ref[...] + part

        @pl.when(k > 0)
        def _():
            o_ref[...] += part
    elif epilogue == "residual":
        x_new = extra_ref[...] + part
        o_ref[...] = x_new
        if norm_out:
            emit_norm_input(x_new)
    elif epilogue == "relu2":
        o_ref[...] = jnp.square(jnp.maximum(part, 0.0)).astype(o_ref.dtype)
    elif epilogue == "headnorm":
        o_ref[...] = _group_rms(part, extra_ref[...], group).astype(o_ref.dtype)
    else:
        o_ref[...] = part.astype(o_ref.dtype)


def matmul(a_list, w, layer, *, tm, tn, tk=None, out_dtype=BF16, epilogue="cast", extra=None, group=None,
           side=None, row_ssq=None, next_gain=None, name="mm"):
    n_a = len(a_list)
    m, kc = a_list[0].shape
    _, k_total, n = w.shape
    assert kc * n_a == k_total
    tm = min(tm, m)
    tn = min(tn, n)
    tk = kc if tk is None else min(tk, kc)
    assert m % tm == 0 and n % tn == 0 and kc % tk == 0
    kt = kc // tk
    assert kt == 1 or (n_a == 1 and epilogue == "residual" and out_dtype == F32)
    kblocks_per_chunk = kc // tk
    grid = (m // tm, n // tn, kt)

    in_specs = []
    for c in range(n_a):
        in_specs.append(pl.BlockSpec((tm, tk), lambda i, j, k: (i, k)))
    for c in range(n_a):
        in_specs.append(pl.BlockSpec((None, tk, tn), lambda i, j, k, c=c: (layer, c * kblocks_per_chunk + k, j)))
    args = list(a_list) + [w] * n_a
    out_bytes = jnp.dtype(out_dtype).itemsize
    vmem = 2 * n_a * (tm * tk + tk * tn) * 2 + 2 * tm * tn * out_bytes + tm * tn * 4
    if epilogue == "residual":
        in_specs.append(pl.BlockSpec((tm, tn), lambda i, j, k: (i, j)))
        args.append(extra)
        vmem += 2 * tm * tn * 4
    elif epilogue == "headnorm":
        assert tn % group == 0
        in_specs.append(pl.BlockSpec((1, group), lambda i, j, k: (0, 0)))
        args.append(extra.reshape(1, group))
    if row_ssq is not None:
        assert kt == 1 and n_a == 1 and row_ssq.shape[0] == m
        in_specs.append(pl.BlockSpec((tm, row_ssq.shape[1]), lambda i, j, k: (i, 0)))
        args.append(row_ssq)
        vmem += 2 * tm * row_ssq.shape[1] * 4
    out_specs = [pl.BlockSpec((tm, tn), lambda i, j, k: (i, j))]
    out_shape = [jax.ShapeDtypeStruct((m, n), out_dtype)]
    if next_gain is not None:
        assert kt == 1 and epilogue == "residual"
        in_specs.append(pl.BlockSpec((1, tn), lambda i, j, k: (0, j)))
        args.append(next_gain.reshape(1, n))
        out_specs += [pl.BlockSpec((tm, tn), lambda i, j, k: (i, j)),
                      pl.BlockSpec((tm, V7X_LANES), lambda i, j, k: (i, j))]
        out_shape += [jax.ShapeDtypeStruct((m, n), BF16),
                      jax.ShapeDtypeStruct((m, (n // tn) * V7X_LANES), F32)]
        vmem += 2 * tm * tn * 2 + 2 * tm * V7X_LANES * 4
    side_blocks = 0
    semantics = ("parallel", "parallel", "arbitrary")
    if side is not None:
        src, src_layer = side
        _, r, c = src.shape
        steps = grid[0] * grid[1] * grid[2]
        side_blocks = 1 << (steps.bit_length() - 1)
        rows_blk = r // side_blocks
        assert rows_blk * side_blocks == r and rows_blk % 16 == 0

        def side_block(i, j, k):
            return jnp.minimum((i * grid[1] + j) * grid[2] + k, side_blocks - 1)

        in_specs.append(pl.BlockSpec((None, rows_blk, c), lambda i, j, k: (src_layer, side_block(i, j, k), 0)))
        args.append(src)
        out_specs.append(pl.BlockSpec((None, rows_blk, c), lambda i, j, k: (0, side_block(i, j, k), 0)))
        out_shape.append(jax.ShapeDtypeStruct((1, r, c), BF16))
        vmem += 2 * rows_blk * c * 6
        semantics = ("arbitrary", "arbitrary", "arbitrary")
    res = pl.pallas_call(
        functools.partial(_mm_kernel, n_a=n_a, kt=kt, epilogue=epilogue, group=group,
                          side_blocks=side_blocks, steps_jk=grid[1:],
                          row_scale_dim=0 if row_ssq is None else k_total, norm_out=next_gain is not None),
        grid=grid,
        in_specs=in_specs,
        out_specs=out_specs,
        out_shape=out_shape,
        compiler_params=pltpu.CompilerParams(
            dimension_semantics=semantics,
            vmem_limit_bytes=int(min(max(vmem + (6 << 20), 32 << 20), VMEM_LIMIT_CAP))),
        name=name,
    )(*args)
    return res if len(res) > 1 else res[0]


def _conv_kernel(ax_ref, ab_ref, ac_ref, buf_ref, w_ref, y_ref, new_ref, carry_ref, *, last_valid):
    t = pl.program_id(1)
    nt = pl.num_programs(1)
    rows = ax_ref.shape[0]
    u = ac_ref[...].astype(F32) * ax_ref[...].astype(F32)

    @pl.when(t == 0)
    def _():
        carry_ref[...] = buf_ref[0]

    prev = carry_ref[...]
    row = lax.broadcasted_iota(jnp.int32, u.shape, 0)
    u1 = jnp.where(row == 0, prev[1:2], pltpu.roll(u, 1, 0))
    u2 = jnp.where(row == 0, prev[0:1], jnp.where(row == 1, prev[1:2], pltpu.roll(u, 2, 0)))
    w = w_ref[...]
    conv = w[0:1] * u2 + w[1:2] * u1 + w[2:3] * u
    y_ref[...] = (ab_ref[...].astype(F32) * conv).astype(y_ref.dtype)
    carry_ref[...] = u[rows - 2:rows]

    @pl.when(t == nt - 1)
    def _():
        new_ref[0] = u[last_valid - 2:last_valid]


def short_conv(z, conv_buf, conv_w, *, batch, rows_per_seq, rows_blk, valid_rows, col_blocks):
    width = conv_w.shape[1]
    rows_blk = min(rows_blk, rows_per_seq)
    nt = rows_per_seq // rows_blk
    last_valid = valid_rows - (nt - 1) * rows_blk
    assert last_valid >= 2
    cx, cb, cc = col_blocks

    def zspec(cblk):
        return pl.BlockSpec((rows_blk, width), lambda b, t, cblk=cblk: (b * nt + t, cblk))

    return pl.pallas_call(
        functools.partial(_conv_kernel, last_valid=last_valid),
        grid=(batch, nt),
        in_specs=[zspec(cx), zspec(cb), zspec(cc),
                  pl.BlockSpec((1, 2, width), lambda b, t: (b, 0, 0)),
                  pl.BlockSpec((3, width), lambda b, t: (0, 0))],
        out_specs=[pl.BlockSpec((rows_blk, width), lambda b, t: (b * nt + t, 0)),
                   pl.BlockSpec((1, 2, width), lambda b, t: (b, 0, 0))],
        out_shape=[jax.ShapeDtypeStruct((batch * rows_per_seq, width), BF16),
                   jax.ShapeDtypeStruct((batch, 2, width), F32)],
        scratch_shapes=[pltpu.VMEM((2, width), F32)],
        compiler_params=_params(("parallel", "arbitrary"), rows_blk * width * 40),
        name="short_conv",
    )(z, z, z, conv_buf, conv_w)


def _cumsum_rows(x):
    rows = x.shape[0]
    row = lax.broadcasted_iota(jnp.int32, x.shape, 0)
    shift = 1
    while shift < rows:
        x = x + jnp.where(row >= shift, pltpu.roll(x, shift, 0), 0.0)
        shift *= 2
    return x


def _log_sigmoid(x):
    return jnp.minimum(x, 0.0) - jnp.log1p(jnp.exp(-jnp.abs(x)))


def _mlstm_kernel(q_ref, k_ref, v_ref, og_ref, g_ref, gbias_ref, gain_ref, c0_ref, n0_ref, m0_ref,
                  y_ref, c_ref, n_ref, m_ref, *, valid, heads, dk, dv):
    chunk = pl.program_id(1)
    L = CHUNK_ROWS
    lb = q_ref.shape[0]

    @pl.when(chunk == 0)
    def _():
        c_ref[...] = c0_ref[...]
        n_ref[...] = n0_ref[...]
        m_ref[...] = m0_ref[...]

    row = lax.broadcasted_iota(jnp.int32, (L, 1), 0)
    tri = lax.broadcasted_iota(jnp.int32, (L, L), 1) <= lax.broadcasted_iota(jnp.int32, (L, L), 0)
    row_ok = row < valid
    gates = _pad_rows(g_ref[...], L) + gbias_ref[...]
    ig_all = jnp.where(row_ok, gates, NEG)
    lf_all = jnp.where(row_ok, _log_sigmoid(gates), 0.0)
    b_all = _cumsum_rows(lf_all)
    ig_all_t = ig_all.T
    b_all_t = b_all.T
    scale = dk ** -0.5
    ys = []
    for h in range(heads):
        q = _pad_rows(q_ref[:, h * dk:(h + 1) * dk], L)
        k = _pad_rows(k_ref[:, h * dk:(h + 1) * dk], L)
        v = _pad_rows(v_ref[:, h * dv:(h + 1) * dv], L)
        ig_c = ig_all[:, h:h + 1]
        b_c = b_all[:, heads + h:heads + h + 1]
        ig_r = ig_all_t[h:h + 1, :]
        b_r = b_all_t[heads + h:heads + h + 1, :]
        c0 = c_ref[0, h]
        n0 = n_ref[0, h:h + 1, :]
        m0 = m_ref[0, :, h:h + 1]

        dmat = jnp.where(tri, b_c + (ig_r - b_r), NEG)
        inter = b_c + m0
        m_c = jnp.maximum(inter, jnp.max(dmat, axis=-1, keepdims=True))
        w_inter = jnp.exp(inter - m_c)
        qk = _nt_dot(q, k) * scale
        w_intra = jnp.exp(dmat - m_c) * qk
        num = w_inter * jnp.dot(q, c0.astype(BF16), preferred_element_type=F32)
        num = num + jnp.dot(w_intra.astype(BF16), v, preferred_element_type=F32)
        qn0 = jnp.sum(q.astype(F32) * n0, axis=-1, keepdims=True)
        den = w_inter * qn0 + jnp.sum(w_intra, axis=-1, keepdims=True)
        hid = num / jnp.maximum(jnp.abs(den), jnp.exp(-m_c))

        b_end = b_c[valid - 1:valid, :]
        m_end = m_c[valid - 1:valid, :]
        g_state = jnp.exp(b_end + m0 - m_end)
        g_tok = jnp.exp(b_end - b_c + ig_c - m_end) * scale
        kg = k.astype(F32) * g_tok
        c_ref[0, h] = g_state * c0 + jnp.dot(kg.T.astype(BF16), v, preferred_element_type=F32)
        n_ref[0, h:h + 1, :] = g_state * n0 + jnp.sum(kg, axis=0, keepdims=True)
        m_ref[0, :, h:h + 1] = m_end

        hn = _group_rms(hid, gain_ref[:, h * dv:(h + 1) * dv], dv)
        og = _pad_rows(og_ref[:, h * dv:(h + 1) * dv], L).astype(F32)
        ys.append(hn * jax.nn.sigmoid(og))
    y = jnp.concatenate(ys, axis=1)
    y_ref[...] = y[:lb].astype(y_ref.dtype)


def mlstm_mixer(z, gates, gate_bias, gain, c0, n0, m0, *, batch, rows_per_seq, valid_rows, cols):
    _, heads, dk, dv = c0.shape
    lb = min(CHUNK_ROWS, rows_per_seq)
    nc = rows_per_seq // lb
    assert rows_per_seq % lb == 0
    valid = valid_rows - (nc - 1) * lb
    assert (nc == 1 or valid == lb) and 1 <= valid <= lb
    q_off, k_off, v_off, o_off = cols
    wq, wv = heads * dk, heads * dv

    def zspec(width, off):
        assert off % width == 0
        return pl.BlockSpec((lb, width), lambda b, c, blk=off // width: (b * nc + c, blk))

    m0 = m0.reshape(batch, 1, heads)
    y, c, n, m = pl.pallas_call(
        functools.partial(_mlstm_kernel, valid=valid, heads=heads, dk=dk, dv=dv),
        grid=(batch, nc),
        in_specs=[zspec(wq, q_off), zspec(wq, k_off), zspec(wv, v_off), zspec(wv, o_off),
                  pl.BlockSpec((lb, V7X_LANES), lambda b, c: (b * nc + c, 0)),
                  pl.BlockSpec((1, V7X_LANES), lambda b, c: (0, 0)),
                  pl.BlockSpec((1, wv), lambda b, c: (0, 0)),
                  pl.BlockSpec((1, heads, dk, dv), lambda b, c: (b, 0, 0, 0)),
                  pl.BlockSpec((1, heads, dk), lambda b, c: (b, 0, 0)),
                  pl.BlockSpec((1, 1, heads), lambda b, c: (b, 0, 0))],
        out_specs=[pl.BlockSpec((lb, wv), lambda b, c: (b * nc + c, 0)),
                   pl.BlockSpec((1, heads, dk, dv), lambda b, c: (b, 0, 0, 0)),
                   pl.BlockSpec((1, heads, dk), lambda b, c: (b, 0, 0)),
                   pl.BlockSpec((1, 1, heads), lambda b, c: (b, 0, 0))],
        out_shape=[jax.ShapeDtypeStruct((batch * rows_per_seq, wv), BF16),
                   jax.ShapeDtypeStruct(c0.shape, F32),
                   jax.ShapeDtypeStruct(n0.shape, F32),
                   jax.ShapeDtypeStruct((batch, 1, heads), F32)],
        compiler_params=_params(("parallel", "arbitrary"), 16 << 20),
        name="mlstm",
    )(z, z, z, z, gates, gate_bias, gain.reshape(1, wv), c0, n0, m0)
    return y, c, n, m.reshape(batch, heads)


def _retention_kernel(q_ref, k_ref, v_ref, gd_ref, gain_ref, s0_ref, y_ref, s_ref, *, valid, heads, dk, dv):
    chunk = pl.program_id(1)
    L = CHUNK_ROWS
    lb = q_ref.shape[0]

    @pl.when(chunk == 0)
    def _():
        s_ref[...] = s0_ref[...]

    row = lax.broadcasted_iota(jnp.int32, (L, 1), 0)
    t_idx = lax.broadcasted_iota(jnp.int32, (L, L), 0)
    s_idx = lax.broadcasted_iota(jnp.int32, (L, L), 1)
    dist = (t_idx - s_idx).astype(F32)
    keep = (t_idx >= s_idx) & (s_idx < valid)
    rowf = row.astype(F32)
    scale = dk ** -0.5
    ys = []
    for h in range(heads):
        log_gamma = math.log1p(-(2.0 ** (-5.0 - h)))
        q = _pad_rows(q_ref[:, h * dk:(h + 1) * dk], L)
        k = _pad_rows(k_ref[:, h * dk:(h + 1) * dk], L)
        v = _pad_rows(v_ref[:, h * dv:(h + 1) * dv], L)
        s0 = s_ref[0, h]
        decay = jnp.where(keep, jnp.exp(jnp.maximum(dist, 0.0) * log_gamma), 0.0)
        smat = _nt_dot(q, k) * scale * decay
        inter = jnp.exp((rowf + 1.0) * log_gamma)
        o = jnp.dot(smat.astype(BF16), v, preferred_element_type=F32)
        o = o + inter * jnp.dot(q, s0.astype(BF16), preferred_element_type=F32)
        tail = jnp.where(row < valid, jnp.exp((valid - 1.0 - rowf) * log_gamma), 0.0) * scale
        kt = k.astype(F32) * tail
        s_ref[0, h] = math.exp(valid * log_gamma) * s0 + jnp.dot(kt.T.astype(BF16), v, preferred_element_type=F32)
        on = _group_rms(o, gain_ref[:, h * dv:(h + 1) * dv], dv)
        gd = _pad_rows(gd_ref[:, h * dv:(h + 1) * dv], L).astype(F32)
        ys.append(on * (gd * jax.nn.sigmoid(gd)))
    y = jnp.concatenate(ys, axis=1)
    y_ref[...] = y[:lb].astype(y_ref.dtype)


def retention_mixer(z, gain, s0, *, batch, rows_per_seq, valid_rows, cols):
    _, heads, dk, dv = s0.shape
    lb = min(CHUNK_ROWS, rows_per_seq)
    nc = rows_per_seq // lb
    valid = valid_rows - (nc - 1) * lb
    assert (nc == 1 or valid == lb) and 1 <= valid <= lb
    q_off, k_off, v_off, g_off = cols
    wq, wv = heads * dk, heads * dv

    def zspec(width, off):
        assert off % width == 0
        return pl.BlockSpec((lb, width), lambda b, c, blk=off // width: (b * nc + c, blk))

    return pl.pallas_call(
        functools.partial(_retention_kernel, valid=valid, heads=heads, dk=dk, dv=dv),
        grid=(batch, nc),
        in_specs=[zspec(wq, q_off), zspec(wq, k_off), zspec(wv, v_off), zspec(wv, g_off),
                  pl.BlockSpec((1, wv), lambda b, c: (0, 0)),
                  pl.BlockSpec((1, heads, dk, dv), lambda b, c: (b, 0, 0, 0))],
        out_specs=[pl.BlockSpec((lb, wv), lambda b, c: (b * nc + c, 0)),
                   pl.BlockSpec((1, heads, dk, dv), lambda b, c: (b, 0, 0, 0))],
        out_shape=[jax.ShapeDtypeStruct((batch * rows_per_seq, wv), BF16),
                   jax.ShapeDtypeStruct(s0.shape, F32)],
        compiler_params=_params(("parallel", "arbitrary"), 16 << 20),
        name="retention",
    )(z, z, z, z, gain.reshape(1, wv), s0)


def _store_cache_rows(ref, x, heads):
    rows, width = x.shape
    dv = width // heads
    tiles_per_head = dv // V7X_LANES
    step = heads * tiles_per_head
    for h in range(heads):
        for t in range(tiles_per_head):
            c0 = h * dv + t * V7X_LANES
            ref[pl.ds(t * heads + h, rows, stride=step), :] = x[:, c0:c0 + V7X_LANES]


def _load_cache_head(ref3, h, heads, tokens, dv):
    tiles_per_head = dv // V7X_LANES
    step = heads * tiles_per_head
    return jnp.concatenate([ref3[0, pl.ds(t * heads + h, tokens, stride=step), :]
                            for t in range(tiles_per_head)], axis=1)


def _cache_view(a, lead):
    tokens, heads, dv = a.shape[-3:]
    t = dv // V7X_LANES
    return (a.reshape(lead, tokens, heads, t, V7X_LANES).transpose(0, 1, 3, 2, 4)
            .reshape(lead, tokens * t * heads, V7X_LANES))


def _cache_unview(a, batch, tokens, heads, dv):
    t = dv // V7X_LANES
    return (a.reshape(batch, tokens, t, heads, V7X_LANES).transpose(0, 1, 3, 2, 4)
            .reshape(batch, tokens, heads, dv))


def _diff_prep_kernel(q_ref, k_ref, v_ref, gq_ref, gk_ref, qn_ref, kb_ref, kc_ref, vc_ref, *, dh, heads):
    qn_ref[...] = (_group_rms(q_ref[...].astype(F32), gq_ref[...], dh) * dh ** -0.5).astype(qn_ref.dtype)
    kn = _group_rms(k_ref[...].astype(F32), gk_ref[...], dh)
    kb_ref[...] = kn.astype(kb_ref.dtype)
    _store_cache_rows(kc_ref, kn, heads)
    _store_cache_rows(vc_ref, v_ref[...].astype(F32), heads)


def diff_prep(z, g_q, g_k, *, rows_blk, cols, width, heads):
    m = z.shape[0]
    dh = g_q.shape[0]
    rows_blk = min(rows_blk, m)
    q_off, k_off, v_off = cols
    il = width // V7X_LANES

    def zspec(off):
        assert off % width == 0
        return pl.BlockSpec((rows_blk, width), lambda i, blk=off // width: (i, blk))

    ospec = pl.BlockSpec((rows_blk, width), lambda i: (i, 0))
    cspec = pl.BlockSpec((rows_blk * il, V7X_LANES), lambda i: (i, 0))
    return pl.pallas_call(
        functools.partial(_diff_prep_kernel, dh=dh, heads=heads),
        grid=(m // rows_blk,),
        in_specs=[zspec(q_off), zspec(k_off), zspec(v_off),
                  pl.BlockSpec((1, dh), lambda i: (0, 0)), pl.BlockSpec((1, dh), lambda i: (0, 0))],
        out_specs=[ospec, ospec, cspec, cspec],
        out_shape=[jax.ShapeDtypeStruct((m, width), BF16),
                   jax.ShapeDtypeStruct((m, width), BF16),
                   jax.ShapeDtypeStruct((m * il, V7X_LANES), F32),
                   jax.ShapeDtypeStruct((m * il, V7X_LANES), F32)],
        compiler_params=_params(("parallel",), rows_blk * width * 48),
        name="diff_prep",
    )(z, z, z, g_q.reshape(1, dh), g_k.reshape(1, dh))


def _softmax_update(s, v, m_ref, l_ref, acc_ref):
    m_old = m_ref[...]
    m_new = jnp.maximum(m_old, jnp.max(s, axis=-1, keepdims=True))
    alpha = jnp.exp(m_old - m_new)
    p = jnp.exp(s - m_new)
    l_ref[...] = alpha * l_ref[...] + jnp.sum(p, axis=-1, keepdims=True)
    acc_ref[...] = alpha * acc_ref[...] + jnp.dot(p.astype(BF16), v, preferred_element_type=F32)
    m_ref[...] = m_new


def _diff_attn_prompt_kernel(slope_ref, lam_ref, q_ref, k_ref, v_ref, gain_ref, y_ref, m_s, l_s, acc_s,
                             *, bq, bk, dh, out_scale, rows_sub):
    h = pl.program_id(1)
    qi = pl.program_id(2)
    ki = pl.program_id(3)

    @pl.when(ki == 0)
    def _():
        m_s[...] = jnp.full(m_s.shape, NEG, F32)
        l_s[...] = jnp.zeros(l_s.shape, F32)
        acc_s[...] = jnp.zeros(acc_s.shape, F32)

    def block_update(diagonal):
        k = k_ref[...]
        v = v_ref[...]
        rel = lax.broadcasted_iota(jnp.int32, (1, bk), 1) + (ki * bk - qi * bq)
        bias = slope_ref[h] * rel.astype(F32)
        for r in range(bq // rows_sub):
            rows = pl.ds(r * rows_sub, rows_sub)
            if diagonal:
                qrow = lax.broadcasted_iota(jnp.int32, (rows_sub, 1), 0) + r * rows_sub
                row_bias = jnp.where(rel <= qrow, bias, NEG)
            else:
                row_bias = bias
            for j in range(2):
                s = _nt_dot(q_ref[rows, j * dh:(j + 1) * dh], k[:, j * dh:(j + 1) * dh]) + row_bias
                _softmax_update(s, v, m_s.at[j, rows], l_s.at[j, rows], acc_s.at[j, rows])

    @pl.when(ki < qi)
    def _():
        block_update(False)

    @pl.when(ki == qi)
    def _():
        block_update(True)
        o = acc_s[0] / l_s[0] - lam_ref[0] * (acc_s[1] / l_s[1])
        y_ref[...] = (_group_rms(o, gain_ref[...], o.shape[1]) * out_scale).astype(y_ref.dtype)


def diff_attention_prompt(qn, kn, z, slopes, lam, gain, *, batch, seq, heads, dh, dv, v_off, blk, out_scale):
    bq = bk = min(blk, seq)
    nq = seq // bq
    hw = 2 * dh
    assert hw == dv and v_off % dv == 0
    vblk = v_off // dv

    def kv_row(b, qi, ki):
        return b * nq + jnp.minimum(ki, qi)

    grid_spec = pltpu.PrefetchScalarGridSpec(
        num_scalar_prefetch=0,
        grid=(batch, heads, nq, nq),
        in_specs=[pl.BlockSpec(memory_space=pltpu.SMEM),
                  pl.BlockSpec(memory_space=pltpu.SMEM),
                  pl.BlockSpec((bq, hw), lambda b, h, qi, ki: (b * nq + qi, h)),
                  pl.BlockSpec((bk, hw), lambda b, h, qi, ki: (kv_row(b, qi, ki), h)),
                  pl.BlockSpec((bk, dv), lambda b, h, qi, ki: (kv_row(b, qi, ki), vblk + h)),
                  pl.BlockSpec((1, dv), lambda b, h, qi, ki: (0, h))],
        out_specs=pl.BlockSpec((bq, dv), lambda b, h, qi, ki: (b * nq + qi, h)),
        scratch_shapes=[pltpu.VMEM((2, bq, 1), F32), pltpu.VMEM((2, bq, 1), F32), pltpu.VMEM((2, bq, dv), F32)],
    )
    return pl.pallas_call(
        functools.partial(_diff_attn_prompt_kernel, bq=bq, bk=bk, dh=dh, out_scale=out_scale,
                          rows_sub=min(256, bq)),
        grid_spec=grid_spec,
        out_shape=jax.ShapeDtypeStruct((batch * seq, heads * dv), BF16),
        compiler_params=_params(("parallel", "parallel", "parallel", "arbitrary"), 24 << 20),
        name="diff_attn_prompt",
    )(slopes, lam, qn, kn, z, gain.reshape(1, heads * dv))


def _diff_attn_decode_kernel(pt_ref, slope_ref, lam_ref, q_ref, kn_ref, vn_ref, gain_ref, *rest,
                             pages, page, past, valid, heads, dh, dv, out_scale):
    k_refs = rest[:pages]
    v_refs = rest[pages:2 * pages]
    y_ref, m_s, l_s, acc_s = rest[2 * pages:]
    step = pl.program_id(1)
    nsteps = pl.num_programs(1)
    rows = q_ref.shape[0]
    hw = 2 * dh

    @pl.when(step == 0)
    def _():
        m_s[...] = jnp.full(m_s.shape, NEG, F32)
        l_s[...] = jnp.zeros(l_s.shape, F32)
        acc_s[...] = jnp.zeros(acc_s.shape, F32)

    def stacked_queries(h):
        qh = q_ref[:, h * hw:(h + 1) * hw]
        zero = jnp.zeros((rows, dh), qh.dtype)
        return jnp.concatenate([jnp.concatenate([qh[:, :dh], zero], axis=1),
                                jnp.concatenate([zero, qh[:, dh:]], axis=1)], axis=0)

    nkeys = pages * page
    rel = lax.broadcasted_iota(jnp.int32, (1, nkeys), 1) + (step * nkeys - past)
    for h in range(heads):
        qs = stacked_queries(h)
        k = jnp.concatenate([_load_cache_head(r, h, heads, page, hw) for r in k_refs], axis=0).astype(BF16)
        v = jnp.concatenate([_load_cache_head(r, h, heads, page, dv) for r in v_refs], axis=0).astype(BF16)
        s = _nt_dot(qs, k) + slope_ref[h] * rel.astype(F32)
        _softmax_update(s, v, m_s.at[h], l_s.at[h], acc_s.at[h])

    @pl.when(step == nsteps - 1)
    def _():
        nk = V7X_LANES
        key = lax.broadcasted_iota(jnp.int32, (1, nk), 1)
        tok = lax.broadcasted_iota(jnp.int32, (2 * rows, 1), 0) & (rows - 1)
        ok = (key <= tok) & (key < valid)
        for h in range(heads):
            qs = stacked_queries(h)
            k = _pad_rows(kn_ref[:, h * hw:(h + 1) * hw], nk)
            v = _pad_rows(vn_ref[:, h * dv:(h + 1) * dv], nk)
            s = _nt_dot(qs, k) + slope_ref[h] * key.astype(F32)
            s = jnp.where(ok, s, NEG)
            _softmax_update(s, v, m_s.at[h], l_s.at[h], acc_s.at[h])
            o = acc_s[h] / l_s[h]
            o = o[:rows] - lam_ref[0] * o[rows:]
            y_ref[:, h * dv:(h + 1) * dv] = (
                _group_rms(o, gain_ref[:, h * dv:(h + 1) * dv], dv) * out_scale).astype(y_ref.dtype)


def diff_attention_decode(qn, kn, z, cache_k, cache_v, page_table, layer, slopes, lam, gain,
                          *, batch, rows_per_seq, valid_rows, heads, dh, dv, v_off, out_scale):
    depth, n_pool, page = cache_k.shape[:3]
    n_pages = page_table.shape[1]
    pages = PAGES_PER_STEP
    assert n_pages % pages == 0
    nsteps = n_pages // pages
    width = heads * dv
    ck = _cache_view(cache_k, depth * n_pool)
    cv = _cache_view(cache_v, depth * n_pool)
    base = layer * n_pool
    page_rows = page * width // V7X_LANES

    def page_spec(p):
        return pl.BlockSpec((1, page_rows, V7X_LANES),
                            lambda b, s, pt, p=p: (base + pt[b * n_pages + s * pages + p], 0, 0))

    row_spec = pl.BlockSpec((rows_per_seq, width), lambda b, s, pt: (b, 0))
    grid_spec = pltpu.PrefetchScalarGridSpec(
        num_scalar_prefetch=1,
        grid=(batch, nsteps),
        in_specs=[pl.BlockSpec(memory_space=pltpu.SMEM),
                  pl.BlockSpec(memory_space=pltpu.SMEM),
                  row_spec, row_spec,
                  pl.BlockSpec((rows_per_seq, width), lambda b, s, pt: (b, v_off // width)),
                  pl.BlockSpec((1, width), lambda b, s, pt: (0, 0))]
                 + [page_spec(p) for p in range(pages)] + [page_spec(p) for p in range(pages)],
        out_specs=row_spec,
        scratch_shapes=[pltpu.VMEM((heads, 2 * rows_per_seq, 1), F32),
                        pltpu.VMEM((heads, 2 * rows_per_seq, 1), F32),
                        pltpu.VMEM((heads, 2 * rows_per_seq, dv), F32)],
    )
    assert v_off % width == 0
    return pl.pallas_call(
        functools.partial(_diff_attn_decode_kernel, pages=pages, page=page, past=n_pages * page,
                          valid=valid_rows, heads=heads, dh=dh, dv=dv, out_scale=out_scale),
        grid_spec=grid_spec,
        out_shape=jax.ShapeDtypeStruct((batch * rows_per_seq, width), BF16),
        compiler_params=_params(("parallel", "arbitrary"), 4 * pages * page * width * 4 + (8 << 20)),
        name="diff_attn_decode",
    )(page_table.reshape(-1), slopes, lam, qn, kn, z, gain.reshape(1, width),
      *([ck] * pages), *([cv] * pages))


def _cross_attn_kernel(q_ref, mk_ref, mv_ref, o_ref, *, heads, dh, n_mem, cache_layout):
    outs = []
    for h in range(heads):
        q = q_ref[:, h * dh:(h + 1) * dh]
        if cache_layout:
            mk = _load_cache_head(mk_ref, h, heads, n_mem, dh).astype(BF16)
            mv = _load_cache_head(mv_ref, h, heads, n_mem, dh).astype(BF16)
        else:
            mk = mk_ref[:, h * dh:(h + 1) * dh].astype(BF16)
            mv = mv_ref[:, h * dh:(h + 1) * dh].astype(BF16)
        s = _nt_dot(q, mk) * dh ** -0.5
        p = jnp.exp(s - jnp.max(s, axis=-1, keepdims=True))
        o = jnp.dot(p.astype(BF16), mv, preferred_element_type=F32)
        outs.append(o / jnp.sum(p, axis=-1, keepdims=True))
    o_ref[...] = jnp.concatenate(outs, axis=1).astype(o_ref.dtype)


def cross_attention(qx, mk, mv, *, batch, rows_per_seq, n_mem, heads, dh, rows_blk, cache_layer=None):
    width = heads * dh
    rows_blk = min(rows_blk, rows_per_seq)
    nq = rows_per_seq // rows_blk
    if cache_layer is None:
        mem_spec = pl.BlockSpec((n_mem, width), lambda b, i: (b, 0))
    else:
        lead = mk.shape[0] * mk.shape[1]
        mk, mv = _cache_view(mk, lead), _cache_view(mv, lead)
        mem_spec = pl.BlockSpec((1, n_mem * width // V7X_LANES, V7X_LANES),
                                lambda b, i: (cache_layer * batch + b, 0, 0))
    return pl.pallas_call(
        functools.partial(_cross_attn_kernel, heads=heads, dh=dh, n_mem=n_mem,
                          cache_layout=cache_layer is not None),
        grid=(batch, nq),
        in_specs=[pl.BlockSpec((rows_blk, width), lambda b, i: (b * nq + i, 0)), mem_spec, mem_spec],
        out_specs=pl.BlockSpec((rows_blk, width), lambda b, i: (b * nq + i, 0)),
        out_shape=jax.ShapeDtypeStruct((batch * rows_per_seq, width), BF16),
        compiler_params=_params(("parallel", "parallel"), 24 << 20),
        name="cross_attn",
    )(qx, mk, mv)


def kernel(x_prompt, x_sample, cache_diff_k, cache_diff_v, cache_mem_k, cache_mem_v, state_conv, state_mlstm_C, state_mlstm_n, state_mlstm_m, state_ret, page_table, mem_prompt, norm_mix, w_in, conv_w, b_igate, b_fgate, g_mlstm, g_diff_q, g_diff_k, lam_q1, lam_k1, lam_q2, lam_k2, g_diff_out, g_ret, w_out, norm_xattn, norm_mem, w_xq, w_xk, w_xv, g_xq, g_xk, w_xo, norm_mlp, w_up, w_down):
    depth = w_in.shape[0]
    bp, seq, d_model = x_prompt.shape
    bs, dec = x_sample.shape[:2]
    conv_ch = conv_w.shape[2]
    _, _, hb, dkb, dvb = state_mlstm_C.shape
    _, _, hd, dkd, dvd = state_ret.shape
    hc, dvc = cache_diff_v.shape[3:]
    dhc = dvc // 2
    _, _, n_mem, hx, dhx = cache_mem_k.shape
    xw = hx * dhx
    assert seq % CHUNK_ROWS == 0 and dec <= SAMPLE_ROWS

    sizes = (conv_ch, conv_ch, conv_ch, hb * dkb, hb * dkb, hb * dvb, hb * dvb, hb, hb,
             2 * hc * dhc, 2 * hc * dhc, hc * dvc, hd * dkd, hd * dkd, hd * dvd, hd * dvd)
    names = ("a_x", "a_b", "a_c", "q_b", "k_b", "v_b", "o_b", "i_b", "f_b",
             "q_c", "k_c", "v_c", "q_d", "k_d", "v_d", "g_d")
    off, pos = {}, 0
    for nm, sz in zip(names, sizes):
        if nm not in ("i_b", "f_b"):
            off[nm] = pos
            pos += sz

    slopes = (2.0 ** (-8.0 * jnp.arange(1, hc + 1, dtype=F32) / hc)).astype(F32)

    x_p = x_prompt.reshape(bp * seq, d_model)
    x_s = jnp.pad(x_sample, ((0, 0), (0, SAMPLE_ROWS - dec), (0, 0))).reshape(bs * SAMPLE_ROWS, d_model)
    mem_rows = mem_prompt.reshape(bp * n_mem, d_model)

    groups = (
        dict(batch=bp, rows=seq, valid=seq, tm=1024, prompt=True),
        dict(batch=bs, rows=SAMPLE_ROWS, valid=dec, tm=bs * SAMPLE_ROWS, prompt=False),
    )
    xs = [x_p, x_s]
    outs = [dict(conv=[], C=[], n=[], m=[], k=[], v=[], S=[], mk=[], mv=[]) for _ in groups]

    gate_lo = sum(sizes[:7])
    w_main, w_gate = split_w_in(w_in, gate_lo, gate_lo + 2 * hb)
    w_out_b, w_xq_b, w_xk_b, w_xv_b, w_xo_b = (cast_bf16(w) for w in (w_out, w_xq, w_xk, w_xv, w_xo))

    for li in range(depth):
        lam_init = 0.8 - 0.6 * math.exp(-0.3 * li)
        lam = (jnp.exp(jnp.sum(lam_q1[li] * lam_k1[li])) - jnp.exp(jnp.sum(lam_q2[li] * lam_k2[li]))
               + lam_init).reshape(1).astype(F32)
        gate_bias = jnp.pad(jnp.concatenate([b_igate[li], b_fgate[li]]), (0, V7X_LANES - 2 * hb)).reshape(1, V7X_LANES)

        hm = rmsnorm_rows(mem_rows, norm_mem[li], 256)
        mk_p = matmul([hm], w_xk_b, li, tm=1024, tn=1024, out_dtype=F32,
                      epilogue="headnorm", extra=g_xk[li], group=dhx, name="mm_mem_k")
        mv_p = matmul([hm], w_xv_b, li, tm=1024, tn=1024, out_dtype=F32, name="mm_mem_v")

        for gi, g in enumerate(groups):
            x = xs[gi]
            batch, rows, valid, tm = g["batch"], g["rows"], g["valid"], g["tm"]
            if g["prompt"]:
                conv_buf = jnp.zeros((batch, 2, conv_ch), F32)
                c0 = jnp.zeros((batch, hb, dkb, dvb), F32)
                n0 = jnp.zeros((batch, hb, dkb), F32)
                m0 = jnp.zeros((batch, hb), F32)
                s0 = jnp.zeros((batch, hd, dkd, dvd), F32)
                mk, mv, cache_layer = mk_p, mv_p, None
            else:
                conv_buf, c0, n0, m0, s0 = (state_conv[li], state_mlstm_C[li], state_mlstm_n[li],
                                            state_mlstm_m[li], state_ret[li])
                mk, mv, cache_layer = cache_mem_k, cache_mem_v, li

            h = rmsnorm_rows(x, norm_mix[li], 256)
            if g["prompt"]:
                z, w_up_l = matmul([h], w_main, li, tm=tm, tn=1024, side=(w_up, li), name="mm_in")
            else:
                z = matmul([h], w_main, li, tm=tm, tn=1024, name="mm_in")
            gates = matmul([h], w_gate, li, tm=tm, tn=V7X_LANES, out_dtype=F32, name="mm_gate")
            y_a, conv_new = short_conv(z, conv_buf, conv_w[li], batch=batch, rows_per_seq=rows, rows_blk=512,
                                       valid_rows=valid,
                                       col_blocks=(off["a_x"] // conv_ch, off["a_b"] // conv_ch, off["a_c"] // conv_ch))
            y_b, c_new, n_new, m_new = mlstm_mixer(
                z, gates, gate_bias, g_mlstm[li], c0, n0, m0, batch=batch, rows_per_seq=rows, valid_rows=valid,
                cols=(off["q_b"], off["k_b"], off["v_b"], off["o_b"]))
            qn, kb, kc, vc = diff_prep(z, g_diff_q[li], g_diff_k[li], rows_blk=512,
                                       cols=(off["q_c"], off["k_c"], off["v_c"]), width=hc * dvc, heads=hc)
            if g["prompt"]:
                y_c = diff_attention_prompt(qn, kb, z, slopes, lam, g_diff_out[li], batch=batch, seq=rows,
                                            heads=hc, dh=dhc, dv=dvc, v_off=off["v_c"], blk=512,
                                            out_scale=1.0 - lam_init)
            else:
                y_c = diff_attention_decode(qn, kb, z, cache_diff_k, cache_diff_v, page_table, li, slopes, lam,
                                            g_diff_out[li], batch=batch, rows_per_seq=rows, valid_rows=valid,
                                            heads=hc, dh=dhc, dv=dvc, v_off=off["v_c"], out_scale=1.0 - lam_init)
            y_d, s_new = retention_mixer(z, g_ret[li], s0, batch=batch, rows_per_seq=rows, valid_rows=valid,
                                         cols=(off["q_d"], off["k_d"], off["v_d"], off["g_d"]))
            x, xg2, ssq2 = matmul([y_a, y_b, y_c, y_d], w_out_b, li, tm=tm, tn=512, out_dtype=F32,
                                  epilogue="residual", extra=x, next_gain=norm_xattn[li], name="mm_out")

            qx = matmul([xg2], w_xq_b, li, tm=tm, tn=1024, epilogue="headnorm", extra=g_xq[li], group=dhx,
                        row_ssq=ssq2, name="mm_xq")
            ox = cross_attention(qx, mk, mv, batch=batch, rows_per_seq=rows, n_mem=n_mem, heads=hx, dh=dhx,
                                 rows_blk=512, cache_layer=cache_layer)
            x, xg3, ssq3 = matmul([ox], w_xo_b, li, tm=tm, tn=1024, out_dtype=F32, epilogue="residual", extra=x,
                                  next_gain=norm_mlp[li], name="mm_xo")

            if g["prompt"]:
                hid, w_down_l = matmul([xg3], w_up_l, 0, tm=tm, tn=1024, epilogue="relu2", row_ssq=ssq3,
                                       side=(w_down, li), name="mm_up")
            else:
                hid = matmul([xg3], w_up_l, 0, tm=tm, tn=1024, epilogue="relu2", row_ssq=ssq3, name="mm_up")
            x = matmul([hid], w_down_l, 0, tm=tm, tn=1024, tk=4096, out_dtype=F32,
                       epilogue="residual", extra=x, name="mm_down")

            xs[gi] = x
            o = outs[gi]
            o["conv"].append(conv_new); o["C"].append(c_new); o["n"].append(n_new); o["m"].append(m_new)
            o["k"].append(kc); o["v"].append(vc); o["S"].append(s_new)
            if g["prompt"]:
                o["mk"].append(mk_p); o["mv"].append(mv_p)

    op, os_ = outs
    y_prompt = xs[0].reshape(bp, seq, d_model)
    y_sample = xs[1].reshape(bs, SAMPLE_ROWS, d_model)[:, :dec]

    def cache_rows(parts, batch, rows, valid):
        a = jnp.stack([_cache_unview(p, batch, rows, hc, dvc) for p in parts])
        return a if rows == valid else a[:, :, :valid]

    return (y_prompt, y_sample,
            jnp.stack(op["conv"]), jnp.stack(op["C"]), jnp.stack(op["n"]), jnp.stack(op["m"]),
            cache_rows(op["k"], bp, seq, seq), cache_rows(op["v"], bp, seq, seq), jnp.stack(op["S"]),
            jnp.stack(op["mk"]).reshape(depth, bp, n_mem, hx, dhx),
            jnp.stack(op["mv"]).reshape(depth, bp, n_mem, hx, dhx),
            jnp.stack(os_["conv"]), jnp.stack(os_["C"]), jnp.stack(os_["n"]), jnp.stack(os_["m"]),
            cache_rows(os_["k"], bs, SAMPLE_ROWS, dec), cache_rows(os_["v"], bs, SAMPLE_ROWS, dec),
            jnp.stack(os_["S"]))
```

```python
import functools
import math

import jax
import jax.numpy as jnp
from jax import lax
from jax.experimental import pallas as pl
from jax.experimental.pallas import tpu as pltpu

V7X_LANES = 128
V7X_VMEM_BYTES = 64 * 1024 * 1024
VMEM_LIMIT_CAP = V7X_VMEM_BYTES - 6 * 1024 * 1024

EPS = 1e-6
NEG = -1e30
CHUNK_ROWS = 128
SAMPLE_ROWS = 16
PAGES_PER_STEP = 8

BF16 = jnp.bfloat16
F32 = jnp.float32


def _params(semantics, vmem_estimate):
    limit = int(min(max(vmem_estimate * 5 // 4 + (4 << 20), 32 << 20), VMEM_LIMIT_CAP))
    return pltpu.CompilerParams(dimension_semantics=semantics, vmem_limit_bytes=limit)


def _nt_dot(a, b):
    return lax.dot_general(a, b, (((1,), (1,)), ((), ())), preferred_element_type=F32)


def _pad_rows(x, rows):
    if x.shape[0] == rows:
        return x
    return jnp.concatenate([x, jnp.zeros((rows - x.shape[0],) + x.shape[1:], x.dtype)], axis=0)


def _group_rms(x, gain, width):
    parts = []
    for g in range(x.shape[1] // width):
        blk = x[:, g * width:(g + 1) * width]
        ms = jnp.mean(blk * blk, axis=-1, keepdims=True)
        parts.append(blk * lax.rsqrt(ms + EPS) * gain)
    return parts[0] if len(parts) == 1 else jnp.concatenate(parts, axis=1)


def _rmsnorm_kernel(x_ref, g_ref, o_ref):
    x = x_ref[...]
    ms = jnp.mean(x * x, axis=-1, keepdims=True)
    o_ref[...] = (x * lax.rsqrt(ms + EPS) * g_ref[...]).astype(o_ref.dtype)


def rmsnorm_rows(x, g, rows_blk):
    m, d = x.shape
    rows_blk = min(rows_blk, m)
    return pl.pallas_call(
        _rmsnorm_kernel,
        grid=(m // rows_blk,),
        in_specs=[pl.BlockSpec((rows_blk, d), lambda i: (i, 0)),
                  pl.BlockSpec((1, d), lambda i: (0, 0))],
        out_specs=pl.BlockSpec((rows_blk, d), lambda i: (i, 0)),
        out_shape=jax.ShapeDtypeStruct((m, d), BF16),
        compiler_params=_params(("parallel",), rows_blk * d * 12),
        name="rmsnorm",
    )(x, g.reshape(1, d))


def _cast_kernel(x_ref, o_ref):
    o_ref[...] = x_ref[...].astype(o_ref.dtype)


def cast_bf16(w, rows_blk=256, cols_blk=4096):
    d, k, n = w.shape
    tr, tc = min(rows_blk, k), min(cols_blk, n)
    assert k % tr == 0 and n % tc == 0
    spec = pl.BlockSpec((None, tr, tc), lambda l, i, j: (l, i, j))
    return pl.pallas_call(
        _cast_kernel,
        grid=(d, k // tr, n // tc),
        in_specs=[spec], out_specs=spec,
        out_shape=jax.ShapeDtypeStruct(w.shape, BF16),
        compiler_params=_params(("parallel", "parallel", "parallel"), tr * tc * 16),
        name="cast_bf16",
    )(w)


def _transpose_cast_kernel(x_ref, o_ref):
    o_ref[...] = x_ref[0].T.astype(o_ref.dtype)


def split_w_in(w_in, g0, g1, tn=1024, tk=1024):
    d, k, n = w_in.shape
    glen = g1 - g0
    n_main = n - glen
    assert g0 % tn == 0 and n_main % tn == 0 and k % tk == 0 and glen % 8 == 0 and glen <= V7X_LANES
    w_t = jnp.swapaxes(w_in, 1, 2)

    def src_index(l, j, kk):
        row = j * tn + jnp.where(j * tn >= g0, glen, 0)
        return (l, pl.multiple_of(row, 8), pl.multiple_of(kk * tk, V7X_LANES))

    main = pl.pallas_call(
        _transpose_cast_kernel,
        grid=(d, n_main // tn, k // tk),
        in_specs=[pl.BlockSpec((pl.Element(1), pl.Element(tn), pl.Element(tk)), src_index)],
        out_specs=pl.BlockSpec((None, tk, tn), lambda l, j, kk: (l, kk, j)),
        out_shape=jax.ShapeDtypeStruct((d, k, n_main), BF16),
        compiler_params=_params(("parallel", "parallel", "parallel"), tn * tk * 24),
        name="split_w_in",
    )(w_t)
    gate = jnp.pad(w_in[:, :, g0:g1], ((0, 0), (0, 0), (0, V7X_LANES - glen))).astype(BF16)
    return main, gate


def _mm_kernel(*refs, n_a, kt, epilogue, group, side_blocks, steps_jk, row_scale_dim, norm_out):
    a_refs = refs[:n_a]
    w_refs = refs[n_a:2 * n_a]
    pos = 2 * n_a
    extra_ref = ssq_in = gain_next = side_in = None
    if epilogue in ("residual", "headnorm"):
        extra_ref = refs[pos]
        pos += 1
    if row_scale_dim:
        ssq_in = refs[pos]
        pos += 1
    if norm_out:
        gain_next = refs[pos]
        pos += 1
    if side_blocks:
        side_in = refs[pos]
        pos += 1
    o_ref = refs[pos]
    pos += 1
    if norm_out:
        xg_ref, ssq_out = refs[pos], refs[pos + 1]
        pos += 2
    side_out = refs[pos] if side_blocks else None

    if side_blocks:
        step = (pl.program_id(0) * steps_jk[0] + pl.program_id(1)) * steps_jk[1] + pl.program_id(2)

        @pl.when(step < side_blocks)
        def _():
            side_out[...] = side_in[...].astype(side_out.dtype)

    part = jnp.dot(a_refs[0][...], w_refs[0][...], preferred_element_type=F32)
    for a_ref, w_ref in zip(a_refs[1:], w_refs[1:]):
        part = part + jnp.dot(a_ref[...], w_ref[...], preferred_element_type=F32)

    if row_scale_dim:
        ssq = ssq_in[...]
        total = ssq[:, 0:1]
        for g in range(1, ssq.shape[1] // V7X_LANES):
            total = total + ssq[:, g * V7X_LANES:g * V7X_LANES + 1]
        part = part * lax.rsqrt(total * (1.0 / row_scale_dim) + EPS)

    def emit_norm_input(x_new):
        xg_ref[...] = (x_new * gain_next[...]).astype(xg_ref.dtype)
        ssq_out[...] = jnp.broadcast_to(jnp.sum(x_new * x_new, axis=-1, keepdims=True), ssq_out.shape)

    if kt > 1:
        k = pl.program_id(2)

        @pl.when(k == 0)
        def _():
            o_ref[...] = extra_ref[...] + part

        @pl.when(k > 0)
        def _():
            o_ref[...] += part
    elif epilogue == "residual":
        x_new = extra_ref[...] + part
        o_ref[...] = x_new
        if norm_out:
            emit_norm_input(x_new)
    elif epilogue == "relu2":
        o_ref[...] = jnp.square(jnp.maximum(part, 0.0)).astype(o_ref.dtype)
    elif epilogue == "headnorm":
        o_ref[...] = _group_rms(part, extra_ref[...], group).astype(o_ref.dtype)
    else:
        o_ref[...] = part.astype(o_ref.dtype)


def matmul(a_list, w, layer, *, tm, tn, tk=None, out_dtype=BF16, epilogue="cast", extra=None, group=None,
           side=None, row_ssq=None, next_gain=None, name="mm"):
    n_a = len(a_list)
    m, kc = a_list[0].shape
    _, k_total, n = w.shape
    assert kc * n_a == k_total
    tm = min(tm, m)
    tn = min(tn, n)
    tk = kc if tk is None else min(tk, kc)
    assert m % tm == 0 and n % tn == 0 and kc % tk == 0
    kt = kc // tk
    assert kt == 1 or (n_a == 1 and epilogue == "residual" and out_dtype == F32)
    kblocks_per_chunk = kc // tk
    grid = (m // tm, n // tn, kt)

    in_specs = []
    for c in range(n_a):
        in_specs.append(pl.BlockSpec((tm, tk), lambda i, j, k: (i, k)))
    for c in range(n_a):
        in_specs.append(pl.BlockSpec((None, tk, tn), lambda i, j, k, c=c: (layer, c * kblocks_per_chunk + k, j)))
    args = list(a_list) + [w] * n_a
    out_bytes = jnp.dtype(out_dtype).itemsize
    vmem = 2 * n_a * (tm * tk + tk * tn) * 2 + 2 * tm * tn * out_bytes + tm * tn * 4
    if epilogue == "residual":
        in_specs.append(pl.BlockSpec((tm, tn), lambda i, j, k: (i, j)))
        args.append(extra)
        vmem += 2 * tm * tn * 4
    elif epilogue == "headnorm":
        assert tn % group == 0
        in_specs.append(pl.BlockSpec((1, group), lambda i, j, k: (0, 0)))
        args.append(extra.reshape(1, group))
    if row_ssq is not None:
        assert kt == 1 and n_a == 1 and row_ssq.shape[0] == m
        in_specs.append(pl.BlockSpec((tm, row_ssq.shape[1]), lambda i, j, k: (i, 0)))
        args.append(row_ssq)
        vmem += 2 * tm * row_ssq.shape[1] * 4
    out_specs = [pl.BlockSpec((tm, tn), lambda i, j, k: (i, j))]
    out_shape = [jax.ShapeDtypeStruct((m, n), out_dtype)]
    if next_gain is not None:
        assert kt == 1 and epilogue == "residual"
        in_specs.append(pl.BlockSpec((1, tn), lambda i, j, k: (0, j)))
        args.append(next_gain.reshape(1, n))
        out_specs += [pl.BlockSpec((tm, tn), lambda i, j, k: (i, j)),
                      pl.BlockSpec((tm, V7X_LANES), lambda i, j, k: (i, j))]
        out_shape += [jax.ShapeDtypeStruct((m, n), BF16),
                      jax.ShapeDtypeStruct((m, (n // tn) * V7X_LANES), F32)]
        vmem += 2 * tm * tn * 2 + 2 * tm * V7X_LANES * 4
    side_blocks = 0
    semantics = ("parallel", "parallel", "arbitrary")
    if side is not None:
        src, src_layer = side
        _, r, c = src.shape
        steps = grid[0] * grid[1] * grid[2]
        side_blocks = 1 << (steps.bit_length() - 1)
        rows_blk = r // side_blocks
        assert rows_blk * side_blocks == r and rows_blk % 16 == 0

        def side_block(i, j, k):
            return jnp.minimum((i * grid[1] + j) * grid[2] + k, side_blocks - 1)

        in_specs.append(pl.BlockSpec((None, rows_blk, c), lambda i, j, k: (src_layer, side_block(i, j, k), 0)))
        args.append(src)
        out_specs.append(pl.BlockSpec((None, rows_blk, c), lambda i, j, k: (0, side_block(i, j, k), 0)))
        out_shape.append(jax.ShapeDtypeStruct((1, r, c), BF16))
        vmem += 2 * rows_blk * c * 6
        semantics = ("arbitrary", "arbitrary", "arbitrary")
    res = pl.pallas_call(
        functools.partial(_mm_kernel, n_a=n_a, kt=kt, epilogue=epilogue, group=group,
                          side_blocks=side_blocks, steps_jk=grid[1:],
                          row_scale_dim=0 if row_ssq is None else k_total, norm_out=next_gain is not None),
        grid=grid,
        in_specs=in_specs,
        out_specs=out_specs,
        out_shape=out_shape,
        compiler_params=pltpu.CompilerParams(
            dimension_semantics=semantics,
            vmem_limit_bytes=int(min(max(vmem + (6 << 20), 32 << 20), VMEM_LIMIT_CAP))),
        name=name,
    )(*args)
    return res if len(res) > 1 else res[0]


def _conv_kernel(ax_ref, ab_ref, ac_ref, buf_ref, w_ref, y_ref, new_ref, carry_ref, *, last_valid):
    t = pl.program_id(1)
    nt = pl.num_programs(1)
    rows = ax_ref.shape[0]
    u = ac_ref[...].astype(F32) * ax_ref[...].astype(F32)

    @pl.when(t == 0)
    def _():
        carry_ref[...] = buf_ref[0]

    prev = carry_ref[...]
    row = lax.broadcasted_iota(jnp.int32, u.shape, 0)
    u1 = jnp.where(row == 0, prev[1:2], pltpu.roll(u, 1, 0))
    u2 = jnp.where(row == 0, prev[0:1], jnp.where(row == 1, prev[1:2], pltpu.roll(u, 2, 0)))
    w = w_ref[...]
    conv = w[0:1] * u2 + w[1:2] * u1 + w[2:3] * u
    y_ref[...] = (ab_ref[...].astype(F32) * conv).astype(y_ref.dtype)
    carry_ref[...] = u[rows - 2:rows]

    @pl.when(t == nt - 1)
    def _():
        new_ref[0] = u[last_valid - 2:last_valid]


def short_conv(z, conv_buf, conv_w, *, batch, rows_per_seq, rows_blk, valid_rows, col_blocks):
    width = conv_w.shape[1]
    rows_blk = min(rows_blk, rows_per_seq)
    nt = rows_per_seq // rows_blk
    last_valid = valid_rows - (nt - 1) * rows_blk
    assert last_valid >= 2
    cx, cb, cc = col_blocks

    def zspec(cblk):
        return pl.BlockSpec((rows_blk, width), lambda b, t, cblk=cblk: (b * nt + t, cblk))

    return pl.pallas_call(
        functools.partial(_conv_kernel, last_valid=last_valid),
        grid=(batch, nt),
        in_specs=[zspec(cx), zspec(cb), zspec(cc),
                  pl.BlockSpec((1, 2, width), lambda b, t: (b, 0, 0)),
                  pl.BlockSpec((3, width), lambda b, t: (0, 0))],
        out_specs=[pl.BlockSpec((rows_blk, width), lambda b, t: (b * nt + t, 0)),
                   pl.BlockSpec((1, 2, width), lambda b, t: (b, 0, 0))],
        out_shape=[jax.ShapeDtypeStruct((batch * rows_per_seq, width), BF16),
                   jax.ShapeDtypeStruct((batch, 2, width), F32)],
        scratch_shapes=[pltpu.VMEM((2, width), F32)],
        compiler_params=_params(("parallel", "arbitrary"), rows_blk * width * 40),
        name="short_conv",
    )(z, z, z, conv_buf, conv_w)


def _cumsum_rows(x):
    rows = x.shape[0]
    row = lax.broadcasted_iota(jnp.int32, x.shape, 0)
    shift = 1
    while shift < rows:
        x = x + jnp.where(row >= shift, pltpu.roll(x, shift, 0), 0.0)
        shift *= 2
    return x


def _log_sigmoid(x):
    return jnp.minimum(x, 0.0) - jnp.log1p(jnp.exp(-jnp.abs(x)))


def _mlstm_kernel(q_ref, k_ref, v_ref, og_ref, g_ref, gbias_ref, gain_ref, c0_ref, n0_ref, m0_ref,
                  y_ref, c_ref, n_ref, m_ref, *, valid, heads, dk, dv):
    chunk = pl.program_id(1)
    L = CHUNK_ROWS
    lb = q_ref.shape[0]

    @pl.when(chunk == 0)
    def _():
        c_ref[...] = c0_ref[...]
        n_ref[...] = n0_ref[...]
        m_ref[...] = m0_ref[...]

    row = lax.broadcasted_iota(jnp.int32, (L, 1), 0)
    tri = lax.broadcasted_iota(jnp.int32, (L, L), 1) <= lax.broadcasted_iota(jnp.int32, (L, L), 0)
    row_ok = row < valid
    gates = _pad_rows(g_ref[...], L) + gbias_ref[...]
    ig_all = jnp.where(row_ok, gates, NEG)
    lf_all = jnp.where(row_ok, _log_sigmoid(gates), 0.0)
    b_all = _cumsum_rows(lf_all)
    ig_all_t = ig_all.T
    b_all_t = b_all.T
    scale = dk ** -0.5
    ys = []
    for h in range(heads):
        q = _pad_rows(q_ref[:, h * dk:(h + 1) * dk], L)
        k = _pad_rows(k_ref[:, h * dk:(h + 1) * dk], L)
        v = _pad_rows(v_ref[:, h * dv:(h + 1) * dv], L)
        ig_c = ig_all[:, h:h + 1]
        b_c = b_all[:, heads + h:heads + h + 1]
        ig_r = ig_all_t[h:h + 1, :]
        b_r = b_all_t[heads + h:heads + h + 1, :]
        c0 = c_ref[0, h]
        n0 = n_ref[0, h:h + 1, :]
        m0 = m_ref[0, :, h:h + 1]

        dmat = jnp.where(tri, b_c + (ig_r - b_r), NEG)
        inter = b_c + m0
        m_c = jnp.maximum(inter, jnp.max(dmat, axis=-1, keepdims=True))
        w_inter = jnp.exp(inter - m_c)
        qk = _nt_dot(q, k) * scale
        w_intra = jnp.exp(dmat - m_c) * qk
        num = w_inter * jnp.dot(q, c0.astype(BF16), preferred_element_type=F32)
        num = num + jnp.dot(w_intra.astype(BF16), v, preferred_element_type=F32)
        qn0 = jnp.sum(q.astype(F32) * n0, axis=-1, keepdims=True)
        den = w_inter * qn0 + jnp.sum(w_intra, axis=-1, keepdims=True)
        hid = num / jnp.maximum(jnp.abs(den), jnp.exp(-m_c))

        b_end = b_c[valid - 1:valid, :]
        m_end = m_c[valid - 1:valid, :]
        g_state = jnp.exp(b_end + m0 - m_end)
        g_tok = jnp.exp(b_end - b_c + ig_c - m_end) * scale
        kg = k.astype(F32) * g_tok
        c_ref[0, h] = g_state * c0 + jnp.dot(kg.T.astype(BF16), v, preferred_element_type=F32)
        n_ref[0, h:h + 1, :] = g_state * n0 + jnp.sum(kg, axis=0, keepdims=True)
        m_ref[0, :, h:h + 1] = m_end

        hn = _group_rms(hid, gain_ref[:, h * dv:(h + 1) * dv], dv)
        og = _pad_rows(og_ref[:, h * dv:(h + 1) * dv], L).astype(F32)
        ys.append(hn * jax.nn.sigmoid(og))
    y = jnp.concatenate(ys, axis=1)
    y_ref[...] = y[:lb].astype(y_ref.dtype)


def mlstm_mixer(z, gates, gate_bias, gain, c0, n0, m0, *, batch, rows_per_seq, valid_rows, cols):
    _, heads, dk, dv = c0.shape
    lb = min(CHUNK_ROWS, rows_per_seq)
    nc = rows_per_seq // lb
    assert rows_per_seq % lb == 0
    valid = valid_rows - (nc - 1) * lb
    assert (nc == 1 or valid == lb) and 1 <= valid <= lb
    q_off, k_off, v_off, o_off = cols
    wq, wv = heads * dk, heads * dv

    def zspec(width, off):
        assert off % width == 0
        return pl.BlockSpec((lb, width), lambda b, c, blk=off // width: (b * nc + c, blk))

    m0 = m0.reshape(batch, 1, heads)
    y, c, n, m = pl.pallas_call(
        functools.partial(_mlstm_kernel, valid=valid, heads=heads, dk=dk, dv=dv),
        grid=(batch, nc),
        in_specs=[zspec(wq, q_off), zspec(wq, k_off), zspec(wv, v_off), zspec(wv, o_off),
                  pl.BlockSpec((lb, V7X_LANES), lambda b, c: (b * nc + c, 0)),
                  pl.BlockSpec((1, V7X_LANES), lambda b, c: (0, 0)),
                  pl.BlockSpec((1, wv), lambda b, c: (0, 0)),
                  pl.BlockSpec((1, heads, dk, dv), lambda b, c: (b, 0, 0, 0)),
                  pl.BlockSpec((1, heads, dk), lambda b, c: (b, 0, 0)),
                  pl.BlockSpec((1, 1, heads), lambda b, c: (b, 0, 0))],
        out_specs=[pl.BlockSpec((lb, wv), lambda b, c: (b * nc + c, 0)),
                   pl.BlockSpec((1, heads, dk, dv), lambda b, c: (b, 0, 0, 0)),
                   pl.BlockSpec((1, heads, dk), lambda b, c: (b, 0, 0)),
                   pl.BlockSpec((1, 1, heads), lambda b, c: (b, 0, 0))],
        out_shape=[jax.ShapeDtypeStruct((batch * rows_per_seq, wv), BF16),
                   jax.ShapeDtypeStruct(c0.shape, F32),
                   jax.ShapeDtypeStruct(n0.shape, F32),
                   jax.ShapeDtypeStruct((batch, 1, heads), F32)],
        compiler_params=_params(("parallel", "arbitrary"), 16 << 20),
        name="mlstm",
    )(z, z, z, z, gates, gate_bias, gain.reshape(1, wv), c0, n0, m0)
    return y, c, n, m.reshape(batch, heads)


def _retention_kernel(q_ref, k_ref, v_ref, gd_ref, gain_ref, s0_ref, y_ref, s_ref, *, valid, heads, dk, dv):
    chunk = pl.program_id(1)
    L = CHUNK_ROWS
    lb = q_ref.shape[0]

    @pl.when(chunk == 0)
    def _():
        s_ref[...] = s0_ref[...]

    row = lax.broadcasted_iota(jnp.int32, (L, 1), 0)
    t_idx = lax.broadcasted_iota(jnp.int32, (L, L), 0)
    s_idx = lax.broadcasted_iota(jnp.int32, (L, L), 1)
    dist = (t_idx - s_idx).astype(F32)
    keep = (t_idx >= s_idx) & (s_idx < valid)
    rowf = row.astype(F32)
    scale = dk ** -0.5
    ys = []
    for h in range(heads):
        log_gamma = math.log1p(-(2.0 ** (-5.0 - h)))
        q = _pad_rows(q_ref[:, h * dk:(h + 1) * dk], L)
        k = _pad_rows(k_ref[:, h * dk:(h + 1) * dk], L)
        v = _pad_rows(v_ref[:, h * dv:(h + 1) * dv], L)
        s0 = s_ref[0, h]
        decay = jnp.where(keep, jnp.exp(jnp.maximum(dist, 0.0) * log_gamma), 0.0)
        smat = _nt_dot(q, k) * scale * decay
        inter = jnp.exp((rowf + 1.0) * log_gamma)
        o = jnp.dot(smat.astype(BF16), v, preferred_element_type=F32)
        o = o + inter * jnp.dot(q, s0.astype(BF16), preferred_element_type=F32)
        tail = jnp.where(row < valid, jnp.exp((valid - 1.0 - rowf) * log_gamma), 0.0) * scale
        kt = k.astype(F32) * tail
        s_ref[0, h] = math.exp(valid * log_gamma) * s0 + jnp.dot(kt.T.astype(BF16), v, preferred_element_type=F32)
        on = _group_rms(o, gain_ref[:, h * dv:(h + 1) * dv], dv)
        gd = _pad_rows(gd_ref[:, h * dv:(h + 1) * dv], L).astype(F32)
        ys.append(on * (gd * jax.nn.sigmoid(gd)))
    y = jnp.concatenate(ys, axis=1)
    y_ref[...] = y[:lb].astype(y_ref.dtype)


def retention_mixer(z, gain, s0, *, batch, rows_per_seq, valid_rows, cols):
    _, heads, dk, dv = s0.shape
    lb = min(CHUNK_ROWS, rows_per_seq)
    nc = rows_per_seq // lb
    valid = valid_rows - (nc - 1) * lb
    assert (nc == 1 or valid == lb) and 1 <= valid <= lb
    q_off, k_off, v_off, g_off = cols
    wq, wv = heads * dk, heads * dv

    def zspec(width, off):
        assert off % width == 0
        return pl.BlockSpec((lb, width), lambda b, c, blk=off // width: (b * nc + c, blk))

    return pl.pallas_call(
        functools.partial(_retention_kernel, valid=valid, heads=heads, dk=dk, dv=dv),
        grid=(batch, nc),
        in_specs=[zspec(wq, q_off), zspec(wq, k_off), zspec(wv, v_off), zspec(wv, g_off),
                  pl.BlockSpec((1, wv), lambda b, c: (0, 0)),
                  pl.BlockSpec((1, heads, dk, dv), lambda b, c: (b, 0, 0, 0))],
        out_specs=[pl.BlockSpec((lb, wv), lambda b, c: (b * nc + c, 0)),
                   pl.BlockSpec((1, heads, dk, dv), lambda b, c: (b, 0, 0, 0))],
        out_shape=[jax.ShapeDtypeStruct((batch * rows_per_seq, wv), BF16),
                   jax.ShapeDtypeStruct(s0.shape, F32)],
        compiler_params=_params(("parallel", "arbitrary"), 16 << 20),
        name="retention",
    )(z, z, z, z, gain.reshape(1, wv), s0)


def _store_cache_rows(ref, x, heads):
    rows, width = x.shape
    dv = width // heads
    tiles_per_head = dv // V7X_LANES
    step = heads * tiles_per_head
    for h in range(heads):
        for t in range(tiles_per_head):
            c0 = h * dv + t * V7X_LANES
            ref[pl.ds(t * heads + h, rows, stride=step), :] = x[:, c0:c0 + V7X_LANES]


def _load_cache_head(ref3, h, heads, tokens, dv):
    tiles_per_head = dv // V7X_LANES
    step = heads * tiles_per_head
    return jnp.concatenate([ref3[0, pl.ds(t * heads + h, tokens, stride=step), :]
                            for t in range(tiles_per_head)], axis=1)


def _cache_view(a, lead):
    tokens, heads, dv = a.shape[-3:]
    t = dv // V7X_LANES
    return (a.reshape(lead, tokens, heads, t, V7X_LANES).transpose(0, 1, 3, 2, 4)
            .reshape(lead, tokens * t * heads, V7X_LANES))


def _cache_unview(a, batch, tokens, heads, dv):
    t = dv // V7X_LANES
    return (a.reshape(batch, tokens, t, heads, V7X_LANES).transpose(0, 1, 3, 2, 4)
            .reshape(batch, tokens, heads, dv))


def _diff_prep_kernel(q_ref, k_ref, v_ref, gq_ref, gk_ref, qn_ref, kb_ref, kc_ref, vc_ref, *, dh, heads):
    qn_ref[...] = (_group_rms(q_ref[...].astype(F32), gq_ref[...], dh) * dh ** -0.5).astype(qn_ref.dtype)
    kn = _group_rms(k_ref[...].astype(F32), gk_ref[...], dh)
    kb_ref[...] = kn.astype(kb_ref.dtype)
    _store_cache_rows(kc_ref, kn, heads)
    _store_cache_rows(vc_ref, v_ref[...].astype(F32), heads)


def diff_prep(z, g_q, g_k, *, rows_blk, cols, width, heads):
    m = z.shape[0]
    dh = g_q.shape[0]
    rows_blk = min(rows_blk, m)
    q_off, k_off, v_off = cols
    il = width // V7X_LANES

    def zspec(off):
        assert off % width == 0
        return pl.BlockSpec((rows_blk, width), lambda i, blk=off // width: (i, blk))

    ospec = pl.BlockSpec((rows_blk, width), lambda i: (i, 0))
    cspec = pl.BlockSpec((rows_blk * il, V7X_LANES), lambda i: (i, 0))
    return pl.pallas_call(
        functools.partial(_diff_prep_kernel, dh=dh, heads=heads),
        grid=(m // rows_blk,),
        in_specs=[zspec(q_off), zspec(k_off), zspec(v_off),
                  pl.BlockSpec((1, dh), lambda i: (0, 0)), pl.BlockSpec((1, dh), lambda i: (0, 0))],
        out_specs=[ospec, ospec, cspec, cspec],
        out_shape=[jax.ShapeDtypeStruct((m, width), BF16),
                   jax.ShapeDtypeStruct((m, width), BF16),
                   jax.ShapeDtypeStruct((m * il, V7X_LANES), F32),
                   jax.ShapeDtypeStruct((m * il, V7X_LANES), F32)],
        compiler_params=_params(("parallel",), rows_blk * width * 48),
        name="diff_prep",
    )(z, z, z, g_q.reshape(1, dh), g_k.reshape(1, dh))


def _softmax_update(s, v, m_ref, l_ref, acc_ref):
    m_old = m_ref[...]
    m_new = jnp.maximum(m_old, jnp.max(s, axis=-1, keepdims=True))
    alpha = jnp.exp(m_old - m_new)
    p = jnp.exp(s - m_new)
    l_ref[...] = alpha * l_ref[...] + jnp.sum(p, axis=-1, keepdims=True)
    acc_ref[...] = alpha * acc_ref[...] + jnp.dot(p.astype(BF16), v, preferred_element_type=F32)
    m_ref[...] = m_new


def _diff_attn_prompt_kernel(qi_tab, ki_tab, slope_ref, lam_ref, q_ref, k_ref, v_ref, gain_ref, y_ref,
                             m_s, l_s, acc_s, *, bq, bk, dh, out_scale, rows_sub):
    h = pl.program_id(1)
    pair = pl.program_id(2)
    qi = qi_tab[pair]
    ki = ki_tab[pair]

    @pl.when(ki == 0)
    def _():
        m_s[...] = jnp.full(m_s.shape, NEG, F32)
        l_s[...] = jnp.zeros(l_s.shape, F32)
        acc_s[...] = jnp.zeros(acc_s.shape, F32)

    def block_update(diagonal):
        rel = lax.broadcasted_iota(jnp.int32, (1, bk), 1) + (ki * bk - qi * bq)
        bias = slope_ref[h] * rel.astype(F32)
        for r in range(bq // rows_sub):
            rows = pl.ds(r * rows_sub, rows_sub)
            nk = min(bk, (r + 1) * rows_sub) if diagonal else bk
            if diagonal:
                qrow = lax.broadcasted_iota(jnp.int32, (rows_sub, 1), 0) + r * rows_sub
                row_bias = jnp.where(rel[:, :nk] <= qrow, bias[:, :nk], NEG)
            else:
                row_bias = bias
            for j in range(2):
                s = _nt_dot(q_ref[rows, j * dh:(j + 1) * dh], k_ref[:nk, j * dh:(j + 1) * dh]) + row_bias
                _softmax_update(s, v_ref[:nk, :], m_s.at[j, rows], l_s.at[j, rows], acc_s.at[j, rows])

    @pl.when(ki < qi)
    def _():
        block_update(False)

    @pl.when(ki == qi)
    def _():
        block_update(True)
        o = acc_s[0] / l_s[0] - lam_ref[0] * (acc_s[1] / l_s[1])
        y_ref[...] = (_group_rms(o, gain_ref[...], o.shape[1]) * out_scale).astype(y_ref.dtype)


def diff_attention_prompt(qn, kn, z, slopes, lam, gain, *, batch, seq, heads, dh, dv, v_off, blk, out_scale):
    bq = bk = min(blk, seq)
    nq = seq // bq
    hw = 2 * dh
    assert hw == dv and v_off % dv == 0
    vblk = v_off // dv
    pairs = [(qi, ki) for qi in range(nq) for ki in range(qi + 1)]
    qi_tab = jnp.asarray([p[0] for p in pairs], jnp.int32)
    ki_tab = jnp.asarray([p[1] for p in pairs], jnp.int32)

    grid_spec = pltpu.PrefetchScalarGridSpec(
        num_scalar_prefetch=2,
        grid=(batch, heads, len(pairs)),
        in_specs=[pl.BlockSpec(memory_space=pltpu.SMEM),
                  pl.BlockSpec(memory_space=pltpu.SMEM),
                  pl.BlockSpec((bq, hw), lambda b, h, p, qt, kt: (b * nq + qt[p], h)),
                  pl.BlockSpec((bk, hw), lambda b, h, p, qt, kt: (b * nq + kt[p], h)),
                  pl.BlockSpec((bk, dv), lambda b, h, p, qt, kt: (b * nq + kt[p], vblk + h)),
                  pl.BlockSpec((1, dv), lambda b, h, p, qt, kt: (0, h))],
        out_specs=pl.BlockSpec((bq, dv), lambda b, h, p, qt, kt: (b * nq + qt[p], h)),
        scratch_shapes=[pltpu.VMEM((2, bq, 1), F32), pltpu.VMEM((2, bq, 1), F32), pltpu.VMEM((2, bq, dv), F32)],
    )
    return pl.pallas_call(
        functools.partial(_diff_attn_prompt_kernel, bq=bq, bk=bk, dh=dh, out_scale=out_scale,
                          rows_sub=min(256, bq)),
        grid_spec=grid_spec,
        out_shape=jax.ShapeDtypeStruct((batch * seq, heads * dv), BF16),
        compiler_params=_params(("parallel", "parallel", "arbitrary"), 24 << 20),
        name="diff_attn_prompt",
    )(qi_tab, ki_tab, slopes, lam, qn, kn, z, gain.reshape(1, heads * dv))


def _diff_attn_decode_kernel(pt_ref, slope_ref, lam_ref, q_ref, kn_ref, vn_ref, gain_ref, *rest,
                             pages, page, past, valid, heads, dh, dv, out_scale):
    k_refs = rest[:pages]
    v_refs = rest[pages:2 * pages]
    y_ref, m_s, l_s, acc_s = rest[2 * pages:]
    step = pl.program_id(1)
    nsteps = pl.num_programs(1)
    rows = q_ref.shape[0]
    hw = 2 * dh

    @pl.when(step == 0)
    def _():
        m_s[...] = jnp.full(m_s.shape, NEG, F32)
        l_s[...] = jnp.zeros(l_s.shape, F32)
        acc_s[...] = jnp.zeros(acc_s.shape, F32)

    def stacked_queries(h):
        qh = q_ref[:, h * hw:(h + 1) * hw]
        zero = jnp.zeros((rows, dh), qh.dtype)
        return jnp.concatenate([jnp.concatenate([qh[:, :dh], zero], axis=1),
                                jnp.concatenate([zero, qh[:, dh:]], axis=1)], axis=0)

    nkeys = pages * page
    rel = lax.broadcasted_iota(jnp.int32, (1, nkeys), 1) + (step * nkeys - past)
    for h in range(heads):
        qs = stacked_queries(h)
        k = jnp.concatenate([_load_cache_head(r, h, heads, page, hw) for r in k_refs], axis=0).astype(BF16)
        v = jnp.concatenate([_load_cache_head(r, h, heads, page, dv) for r in v_refs], axis=0).astype(BF16)
        s = _nt_dot(qs, k) + slope_ref[h] * rel.astype(F32)
        _softmax_update(s, v, m_s.at[h], l_s.at[h], acc_s.at[h])

    @pl.when(step == nsteps - 1)
    def _():
        nk = V7X_LANES
        key = lax.broadcasted_iota(jnp.int32, (1, nk), 1)
        tok = lax.broadcasted_iota(jnp.int32, (2 * rows, 1), 0) & (rows - 1)
        ok = (key <= tok) & (key < valid)
        for h in range(heads):
            qs = stacked_queries(h)
            k = _pad_rows(kn_ref[:, h * hw:(h + 1) * hw], nk)
            v = _pad_rows(vn_ref[:, h * dv:(h + 1) * dv], nk)
            s = _nt_dot(qs, k) + slope_ref[h] * key.astype(F32)
            s = jnp.where(ok, s, NEG)
            _softmax_update(s, v, m_s.at[h], l_s.at[h], acc_s.at[h])
            o = acc_s[h] / l_s[h]
            o = o[:rows] - lam_ref[0] * o[rows:]
            y_ref[:, h * dv:(h + 1) * dv] = (
                _group_rms(o, gain_ref[:, h * dv:(h + 1) * dv], dv) * out_scale).astype(y_ref.dtype)


def diff_attention_decode(qn, kn, z, cache_k, cache_v, page_table, layer, slopes, lam, gain,
                          *, batch, rows_per_seq, valid_rows, heads, dh, dv, v_off, out_scale):
    depth, n_pool, page = cache_k.shape[:3]
    n_pages = page_table.shape[1]
    pages = PAGES_PER_STEP
    assert n_pages % pages == 0
    nsteps = n_pages // pages
    width = heads * dv
    ck = _cache_view(cache_k, depth * n_pool)
    cv = _cache_view(cache_v, depth * n_pool)
    base = layer * n_pool
    page_rows = page * width // V7X_LANES

    def page_spec(p):
        return pl.BlockSpec((1, page_rows, V7X_LANES),
                            lambda b, s, pt, p=p: (base + pt[b * n_pages + s * pages + p], 0, 0))

    row_spec = pl.BlockSpec((rows_per_seq, width), lambda b, s, pt: (b, 0))
    grid_spec = pltpu.PrefetchScalarGridSpec(
        num_scalar_prefetch=1,
        grid=(batch, nsteps),
        in_specs=[pl.BlockSpec(memory_space=pltpu.SMEM),
                  pl.BlockSpec(memory_space=pltpu.SMEM),
                  row_spec, row_spec,
                  pl.BlockSpec((rows_per_seq, width), lambda b, s, pt: (b, v_off // width)),
                  pl.BlockSpec((1, width), lambda b, s, pt: (0, 0))]
                 + [page_spec(p) for p in range(pages)] + [page_spec(p) for p in range(pages)],
        out_specs=row_spec,
        scratch_shapes=[pltpu.VMEM((heads, 2 * rows_per_seq, 1), F32),
                        pltpu.VMEM((heads, 2 * rows_per_seq, 1), F32),
                        pltpu.VMEM((heads, 2 * rows_per_seq, dv), F32)],
    )
    assert v_off % width == 0
    return pl.pallas_call(
        functools.partial(_diff_attn_decode_kernel, pages=pages, page=page, past=n_pages * page,
                          valid=valid_rows, heads=heads, dh=dh, dv=dv, out_scale=out_scale),
        grid_spec=grid_spec,
        out_shape=jax.ShapeDtypeStruct((batch * rows_per_seq, width), BF16),
        compiler_params=_params(("parallel", "arbitrary"), 4 * pages * page * width * 4 + (8 << 20)),
        name="diff_attn_decode",
    )(page_table.reshape(-1), slopes, lam, qn, kn, z, gain.reshape(1, width),
      *([ck] * pages), *([cv] * pages))


def _cross_attn_kernel(q_ref, mk_ref, mv_ref, o_ref, *, heads, dh, n_mem, cache_layout):
    outs = []
    for h in range(heads):
        q = q_ref[:, h * dh:(h + 1) * dh]
        if cache_layout:
            mk = _load_cache_head(mk_ref, h, heads, n_mem, dh).astype(BF16)
            mv = _load_cache_head(mv_ref, h, heads, n_mem, dh).astype(BF16)
        else:
            mk = mk_ref[:, h * dh:(h + 1) * dh].astype(BF16)
            mv = mv_ref[:, h * dh:(h + 1) * dh].astype(BF16)
        s = _nt_dot(q, mk) * dh ** -0.5
        p = jnp.exp(s - jnp.max(s, axis=-1, keepdims=True))
        o = jnp.dot(p.astype(BF16), mv, preferred_element_type=F32)
        outs.append(o / jnp.sum(p, axis=-1, keepdims=True))
    o_ref[...] = jnp.concatenate(outs, axis=1).astype(o_ref.dtype)


def cross_attention(qx, mk, mv, *, batch, rows_per_seq, n_mem, heads, dh, rows_blk, cache_layer=None):
    width = heads * dh
    rows_blk = min(rows_blk, rows_per_seq)
    nq = rows_per_seq // rows_blk
    if cache_layer is None:
        mem_spec = pl.BlockSpec((n_mem, width), lambda b, i: (b, 0))
    else:
        lead = mk.shape[0] * mk.shape[1]
        mk, mv = _cache_view(mk, lead), _cache_view(mv, lead)
        mem_spec = pl.BlockSpec((1, n_mem * width // V7X_LANES, V7X_LANES),
                                lambda b, i: (cache_layer * batch + b, 0, 0))
    return pl.pallas_call(
        functools.partial(_cross_attn_kernel, heads=heads, dh=dh, n_mem=n_mem,
                          cache_layout=cache_layer is not None),
        grid=(batch, nq),
        in_specs=[pl.BlockSpec((rows_blk, width), lambda b, i: (b * nq + i, 0)), mem_spec, mem_spec],
        out_specs=pl.BlockSpec((rows_blk, width), lambda b, i: (b * nq + i, 0)),
        out_shape=jax.ShapeDtypeStruct((batch * rows_per_seq, width), BF16),
        compiler_params=_params(("parallel", "parallel"), 24 << 20),
        name="cross_attn",
    )(qx, mk, mv)


def kernel(x_prompt, x_sample, cache_diff_k, cache_diff_v, cache_mem_k, cache_mem_v, state_conv, state_mlstm_C, state_mlstm_n, state_mlstm_m, state_ret, page_table, mem_prompt, norm_mix, w_in, conv_w, b_igate, b_fgate, g_mlstm, g_diff_q, g_diff_k, lam_q1, lam_k1, lam_q2, lam_k2, g_diff_out, g_ret, w_out, norm_xattn, norm_mem, w_xq, w_xk, w_xv, g_xq, g_xk, w_xo, norm_mlp, w_up, w_down):
    depth = w_in.shape[0]
    bp, seq, d_model = x_prompt.shape
    bs, dec = x_sample.shape[:2]
    conv_ch = conv_w.shape[2]
    _, _, hb, dkb, dvb = state_mlstm_C.shape
    _, _, hd, dkd, dvd = state_ret.shape
    hc, dvc = cache_diff_v.shape[3:]
    dhc = dvc // 2
    _, _, n_mem, hx, dhx = cache_mem_k.shape
    xw = hx * dhx
    assert seq % CHUNK_ROWS == 0 and dec <= SAMPLE_ROWS

    sizes = (conv_ch, conv_ch, conv_ch, hb * dkb, hb * dkb, hb * dvb, hb * dvb, hb, hb,
             2 * hc * dhc, 2 * hc * dhc, hc * dvc, hd * dkd, hd * dkd, hd * dvd, hd * dvd)
    names = ("a_x", "a_b", "a_c", "q_b", "k_b", "v_b", "o_b", "i_b", "f_b",
             "q_c", "k_c", "v_c", "q_d", "k_d", "v_d", "g_d")
    off, pos = {}, 0
    for nm, sz in zip(names, sizes):
        if nm not in ("i_b", "f_b"):
            off[nm] = pos
            pos += sz

    slopes = (2.0 ** (-8.0 * jnp.arange(1, hc + 1, dtype=F32) / hc)).astype(F32)

    x_p = x_prompt.reshape(bp * seq, d_model)
    x_s = jnp.pad(x_sample, ((0, 0), (0, SAMPLE_ROWS - dec), (0, 0))).reshape(bs * SAMPLE_ROWS, d_model)
    mem_rows = mem_prompt.reshape(bp * n_mem, d_model)

    groups = (
        dict(batch=bp, rows=seq, valid=seq, tm=1024, prompt=True),
        dict(batch=bs, rows=SAMPLE_ROWS, valid=dec, tm=bs * SAMPLE_ROWS, prompt=False),
    )
    xs = [x_p, x_s]
    outs = [dict(conv=[], C=[], n=[], m=[], k=[], v=[], S=[], mk=[], mv=[]) for _ in groups]

    gate_lo = sum(sizes[:7])
    w_main, w_gate = split_w_in(w_in, gate_lo, gate_lo + 2 * hb)
    w_out_b, w_xq_b, w_xk_b, w_xv_b, w_xo_b = (cast_bf16(w) for w in (w_out, w_xq, w_xk, w_xv, w_xo))

    for li in range(depth):
        lam_init = 0.8 - 0.6 * math.exp(-0.3 * li)
        lam = (jnp.exp(jnp.sum(lam_q1[li] * lam_k1[li])) - jnp.exp(jnp.sum(lam_q2[li] * lam_k2[li]))
               + lam_init).reshape(1).astype(F32)
        gate_bias = jnp.pad(jnp.concatenate([b_igate[li], b_fgate[li]]), (0, V7X_LANES - 2 * hb)).reshape(1, V7X_LANES)

        hm = rmsnorm_rows(mem_rows, norm_mem[li], 256)
        mk_p = matmul([hm], w_xk_b, li, tm=1024, tn=1024, out_dtype=F32,
                      epilogue="headnorm", extra=g_xk[li], group=dhx, name="mm_mem_k")
        mv_p = matmul([hm], w_xv_b, li, tm=1024, tn=1024, out_dtype=F32, name="mm_mem_v")

        for gi, g in enumerate(groups):
            x = xs[gi]
            batch, rows, valid, tm = g["batch"], g["rows"], g["valid"], g["tm"]
            if g["prompt"]:
                conv_buf = jnp.zeros((batch, 2, conv_ch), F32)
                c0 = jnp.zeros((batch, hb, dkb, dvb), F32)
                n0 = jnp.zeros((batch, hb, dkb), F32)
                m0 = jnp.zeros((batch, hb), F32)
                s0 = jnp.zeros((batch, hd, dkd, dvd), F32)
                mk, mv, cache_layer = mk_p, mv_p, None
            else:
                conv_buf, c0, n0, m0, s0 = (state_conv[li], state_mlstm_C[li], state_mlstm_n[li],
                                            state_mlstm_m[li], state_ret[li])
                mk, mv, cache_layer = cache_mem_k, cache_mem_v, li

            h = rmsnorm_rows(x, norm_mix[li], 256)
            if g["prompt"]:
                z, w_up_l = matmul([h], w_main, li, tm=tm, tn=1024, side=(w_up, li), name="mm_in")
            else:
                z = matmul([h], w_main, li, tm=tm, tn=1024, name="mm_in")
            gates = matmul([h], w_gate, li, tm=tm, tn=V7X_LANES, out_dtype=F32, name="mm_gate")
            y_a, conv_new = short_conv(z, conv_buf, conv_w[li], batch=batch, rows_per_seq=rows, rows_blk=512,
                                       valid_rows=valid,
                                       col_blocks=(off["a_x"] // conv_ch, off["a_b"] // conv_ch, off["a_c"] // conv_ch))
            y_b, c_new, n_new, m_new = mlstm_mixer(
                z, gates, gate_bias, g_mlstm[li], c0, n0, m0, batch=batch, rows_per_seq=rows, valid_rows=valid,
                cols=(off["q_b"], off["k_b"], off["v_b"], off["o_b"]))
            qn, kb, kc, vc = diff_prep(z, g_diff_q[li], g_diff_k[li], rows_blk=512,
                                       cols=(off["q_c"], off["k_c"], off["v_c"]), width=hc * dvc, heads=hc)
            if g["prompt"]:
                y_c = diff_attention_prompt(qn, kb, z, slopes, lam, g_diff_out[li], batch=batch, seq=rows,
                                            heads=hc, dh=dhc, dv=dvc, v_off=off["v_c"], blk=512,
                                            out_scale=1.0 - lam_init)
            else:
                y_c = diff_attention_decode(qn, kb, z, cache_diff_k, cache_diff_v, page_table, li, slopes, lam,
                                            g_diff_out[li], batch=batch, rows_per_seq=rows, valid_rows=valid,
                                            heads=hc, dh=dhc, dv=dvc, v_off=off["v_c"], out_scale=1.0 - lam_init)
            y_d, s_new = retention_mixer(z, g_ret[li], s0, batch=batch, rows_per_seq=rows, valid_rows=valid,
                                         cols=(off["q_d"], off["k_d"], off["v_d"], off["g_d"]))
            x, xg2, ssq2 = matmul([y_a, y_b, y_c, y_d], w_out_b, li, tm=tm, tn=512, out_dtype=F32,
                                  epilogue="residual", extra=x, next_gain=norm_xattn[li], name="mm_out")

            qx = matmul([xg2], w_xq_b, li, tm=tm, tn=1024, epilogue="headnorm", extra=g_xq[li], group=dhx,
                        row_ssq=ssq2, name="mm_xq")
            ox = cross_attention(qx, mk, mv, batch=batch, rows_per_seq=rows, n_mem=n_mem, heads=hx, dh=dhx,
                                 rows_blk=512, cache_layer=cache_layer)
            x, xg3, ssq3 = matmul([ox], w_xo_b, li, tm=tm, tn=1024, out_dtype=F32, epilogue="residual", extra=x,
                                  next_gain=norm_mlp[li], name="mm_xo")

            if g["prompt"]:
                hid, w_down_l = matmul([xg3], w_up_l, 0, tm=tm, tn=1024, epilogue="relu2", row_ssq=ssq3,
                                       side=(w_down, li), name="mm_up")
            else:
                hid = matmul([xg3], w_up_l, 0, tm=tm, tn=1024, epilogue="relu2", row_ssq=ssq3, name="mm_up")
            x = matmul([hid], w_down_l, 0, tm=tm, tn=1024, tk=4096, out_dtype=F32,
                       epilogue="residual", extra=x, name="mm_down")

            xs[gi] = x
            o = outs[gi]
            o["conv"].append(conv_new); o["C"].append(c_new); o["n"].append(n_new); o["m"].append(m_new)
            o["k"].append(kc); o["v"].append(vc); o["S"].append(s_new)
            if g["prompt"]:
                o["mk"].append(mk_p); o["mv"].append(mv_p)

    op, os_ = outs
    y_prompt = xs[0].reshape(bp, seq, d_model)
    y_sample = xs[1].reshape(bs, SAMPLE_ROWS, d_model)[:, :dec]

    def cache_rows(parts, batch, rows, valid):
        a = jnp.stack([_cache_unview(p, batch, rows, hc, dvc) for p in parts])
        return a if rows == valid else a[:, :, :valid]

    return (y_prompt, y_sample,
            jnp.stack(op["conv"]), jnp.stack(op["C"]), jnp.stack(op["n"]), jnp.stack(op["m"]),
            cache_rows(op["k"], bp, seq, seq), cache_rows(op["v"], bp, seq, seq), jnp.stack(op["S"]),
            jnp.stack(op["mk"]).reshape(depth, bp, n_mem, hx, dhx),
            jnp.stack(op["mv"]).reshape(depth, bp, n_mem, hx, dhx),
            jnp.stack(os_["conv"]), jnp.stack(os_["C"]), jnp.stack(os_["n"]), jnp.stack(os_["m"]),
            cache_rows(os_["k"], bs, SAMPLE_ROWS, dec), cache_rows(os_["v"], bs, SAMPLE_ROWS, dec),
            jnp.stack(os_["S"]))
```

```python
import functools
import math

import jax
import jax.numpy as jnp
from jax import lax
from jax.experimental import pallas as pl
from jax.experimental.pallas import tpu as pltpu

V7X_LANES = 128
V7X_VMEM_BYTES = 64 * 1024 * 1024
VMEM_LIMIT_CAP = V7X_VMEM_BYTES - 6 * 1024 * 1024

EPS = 1e-6
NEG = -1e30
CHUNK_ROWS = 128
SAMPLE_ROWS = 16
PAGES_PER_STEP = 16

BF16 = jnp.bfloat16
F32 = jnp.float32


def _params(semantics, vmem_estimate):
    limit = int(min(max(vmem_estimate * 5 // 4 + (4 << 20), 32 << 20), VMEM_LIMIT_CAP))
    return pltpu.CompilerParams(dimension_semantics=semantics, vmem_limit_bytes=limit)


def _nt_dot(a, b):
    return lax.dot_general(a, b, (((1,), (1,)), ((), ())), preferred_element_type=F32)


def _pad_rows(x, rows):
    if x.shape[0] == rows:
        return x
    return jnp.concatenate([x, jnp.zeros((rows - x.shape[0],) + x.shape[1:], x.dtype)], axis=0)


def _group_rms(x, gain, width):
    parts = []
    for g in range(x.shape[1] // width):
        blk = x[:, g * width:(g + 1) * width]
        ms = jnp.mean(blk * blk, axis=-1, keepdims=True)
        parts.append(blk * lax.rsqrt(ms + EPS) * gain)
    return parts[0] if len(parts) == 1 else jnp.concatenate(parts, axis=1)


def _rmsnorm_kernel(x_ref, g_ref, o_ref):
    x = x_ref[...]
    ms = jnp.mean(x * x, axis=-1, keepdims=True)
    o_ref[...] = (x * lax.rsqrt(ms + EPS) * g_ref[...]).astype(o_ref.dtype)


def rmsnorm_rows(x, g, rows_blk):
    m, d = x.shape
    rows_blk = min(rows_blk, m)
    return pl.pallas_call(
        _rmsnorm_kernel,
        grid=(m // rows_blk,),
        in_specs=[pl.BlockSpec((rows_blk, d), lambda i: (i, 0)),
                  pl.BlockSpec((1, d), lambda i: (0, 0))],
        out_specs=pl.BlockSpec((rows_blk, d), lambda i: (i, 0)),
        out_shape=jax.ShapeDtypeStruct((m, d), BF16),
        compiler_params=_params(("parallel",), rows_blk * d * 12),
        name="rmsnorm",
    )(x, g.reshape(1, d))


def _cast_kernel(x_ref, o_ref):
    o_ref[...] = x_ref[...].astype(o_ref.dtype)


def cast_bf16(w, rows_blk=256, cols_blk=4096):
    d, k, n = w.shape
    tr, tc = min(rows_blk, k), min(cols_blk, n)
    assert k % tr == 0 and n % tc == 0
    spec = pl.BlockSpec((None, tr, tc), lambda l, i, j: (l, i, j))
    return pl.pallas_call(
        _cast_kernel,
        grid=(d, k // tr, n // tc),
        in_specs=[spec], out_specs=spec,
        out_shape=jax.ShapeDtypeStruct(w.shape, BF16),
        compiler_params=_params(("parallel", "parallel", "parallel"), tr * tc * 16),
        name="cast_bf16",
    )(w)


def _transpose_cast_kernel(x_ref, o_ref):
    o_ref[...] = x_ref[0].T.astype(o_ref.dtype)


def split_w_in(w_in, g0, g1, tn=1024, tk=1024):
    d, k, n = w_in.shape
    glen = g1 - g0
    n_main = n - glen
    assert g0 % tn == 0 and n_main % tn == 0 and k % tk == 0 and glen % 8 == 0 and glen <= V7X_LANES
    w_t = jnp.swapaxes(w_in, 1, 2)

    def src_index(l, j, kk):
        row = j * tn + jnp.where(j * tn >= g0, glen, 0)
        return (l, pl.multiple_of(row, 8), pl.multiple_of(kk * tk, V7X_LANES))

    main = pl.pallas_call(
        _transpose_cast_kernel,
        grid=(d, n_main // tn, k // tk),
        in_specs=[pl.BlockSpec((pl.Element(1), pl.Element(tn), pl.Element(tk)), src_index)],
        out_specs=pl.BlockSpec((None, tk, tn), lambda l, j, kk: (l, kk, j)),
        out_shape=jax.ShapeDtypeStruct((d, k, n_main), BF16),
        compiler_params=_params(("parallel", "parallel", "parallel"), tn * tk * 24),
        name="split_w_in",
    )(w_t)
    gate = jnp.pad(w_in[:, :, g0:g1], ((0, 0), (0, 0), (0, V7X_LANES - glen))).astype(BF16)
    return main, gate


def _mm_kernel(*refs, n_a, kt, epilogue, group, side_blocks, steps_jk, row_scale_dim, norm_out):
    a_refs = refs[:n_a]
    w_refs = refs[n_a:2 * n_a]
    pos = 2 * n_a
    extra_ref = ssq_in = gain_next = side_in = None
    if epilogue in ("residual", "headnorm"):
        extra_ref = refs[pos]
        pos += 1
    if row_scale_dim:
        ssq_in = refs[pos]
        pos += 1
    if norm_out:
        gain_next = refs[pos]
        pos += 1
    if side_blocks:
        side_in = refs[pos]
        pos += 1
    o_ref = refs[pos]
    pos += 1
    if norm_out:
        xg_ref, ssq_out = refs[pos], refs[pos + 1]
        pos += 2
    side_out = refs[pos] if side_blocks else None

    if side_blocks:
        step = (pl.program_id(0) * steps_jk[0] + pl.program_id(1)) * steps_jk[1] + pl.program_id(2)

        @pl.when(step < side_blocks)
        def _():
            side_out[...] = side_in[...].astype(side_out.dtype)

    part = jnp.dot(a_refs[0][...], w_refs[0][...], preferred_element_type=F32)
    for a_ref, w_ref in zip(a_refs[1:], w_refs[1:]):
        part = part + jnp.dot(a_ref[...], w_ref[...], preferred_element_type=F32)

    if row_scale_dim:
        ssq = ssq_in[...]
        total = ssq[:, 0:1]
        for g in range(1, ssq.shape[1] // V7X_LANES):
            total = total + ssq[:, g * V7X_LANES:g * V7X_LANES + 1]
        part = part * lax.rsqrt(total * (1.0 / row_scale_dim) + EPS)

    def emit_norm_input(x_new):
        xg_ref[...] = (x_new * gain_next[...]).astype(xg_ref.dtype)
        ssq_out[...] = jnp.broadcast_to(jnp.sum(x_new * x_new, axis=-1, keepdims=True), ssq_out.shape)

    if kt > 1:
        k = pl.program_id(2)

        @pl.when(k == 0)
        def _():
            o_ref[...] = extra_ref[...] + part

        @pl.when(k > 0)
        def _():
            o_ref[...] += part
    elif epilogue == "residual":
        x_new = extra_ref[...] + part
        o_ref[...] = x_new
        if norm_out:
            emit_norm_input(x_new)
    elif epilogue == "relu2":
        o_ref[...] = jnp.square(jnp.maximum(part, 0.0)).astype(o_ref.dtype)
    elif epilogue == "headnorm":
        o_ref[...] = _group_rms(part, extra_ref[...], group).astype(o_ref.dtype)
    else:
        o_ref[...] = part.astype(o_ref.dtype)


def matmul(a_list, w, layer, *, tm, tn, tk=None, out_dtype=BF16, epilogue="cast", extra=None, group=None,
           side=None, row_ssq=None, next_gain=None, name="mm"):
    n_a = len(a_list)
    m, kc = a_list[0].shape
    _, k_total, n = w.shape
    assert kc * n_a == k_total
    tm = min(tm, m)
    tn = min(tn, n)
    tk = kc if tk is None else min(tk, kc)
    assert m % tm == 0 and n % tn == 0 and kc % tk == 0
    kt = kc // tk
    assert kt == 1 or (n_a == 1 and epilogue == "residual" and out_dtype == F32)
    kblocks_per_chunk = kc // tk
    grid = (m // tm, n // tn, kt)

    in_specs = []
    for c in range(n_a):
        in_specs.append(pl.BlockSpec((tm, tk), lambda i, j, k: (i, k)))
    for c in range(n_a):
        in_specs.append(pl.BlockSpec((None, tk, tn), lambda i, j, k, c=c: (layer, c * kblocks_per_chunk + k, j)))
    args = list(a_list) + [w] * n_a
    out_bytes = jnp.dtype(out_dtype).itemsize
    vmem = 2 * n_a * (tm * tk + tk * tn) * 2 + 2 * tm * tn * out_bytes + tm * tn * 4
    if epilogue == "residual":
        in_specs.append(pl.BlockSpec((tm, tn), lambda i, j, k: (i, j)))
        args.append(extra)
        vmem += 2 * tm * tn * 4
    elif epilogue == "headnorm":
        assert tn % group == 0
        in_specs.append(pl.BlockSpec((1, group), lambda i, j, k: (0, 0)))
        args.append(extra.reshape(1, group))
    if row_ssq is not None:
        assert kt == 1 and n_a == 1 and row_ssq.shape[0] == m
        in_specs.append(pl.BlockSpec((tm, row_ssq.shape[1]), lambda i, j, k: (i, 0)))
        args.append(row_ssq)
        vmem += 2 * tm * row_ssq.shape[1] * 4
    out_specs = [pl.BlockSpec((tm, tn), lambda i, j, k: (i, j))]
    out_shape = [jax.ShapeDtypeStruct((m, n), out_dtype)]
    if next_gain is not None:
        assert kt == 1 and epilogue == "residual"
        in_specs.append(pl.BlockSpec((1, tn), lambda i, j, k: (0, j)))
        args.append(next_gain.reshape(1, n))
        out_specs += [pl.BlockSpec((tm, tn), lambda i, j, k: (i, j)),
                      pl.BlockSpec((tm, V7X_LANES), lambda i, j, k: (i, j))]
        out_shape += [jax.ShapeDtypeStruct((m, n), BF16),
                      jax.ShapeDtypeStruct((m, (n // tn) * V7X_LANES), F32)]
        vmem += 2 * tm * tn * 2 + 2 * tm * V7X_LANES * 4
    side_blocks = 0
    semantics = ("parallel", "parallel", "arbitrary")
    if side is not None:
        src, src_layer = side
        _, r, c = src.shape
        steps = grid[0] * grid[1] * grid[2]
        side_blocks = 1 << (steps.bit_length() - 1)
        rows_blk = r // side_blocks
        assert rows_blk * side_blocks == r and rows_blk % 16 == 0

        def side_block(i, j, k):
            return jnp.minimum((i * grid[1] + j) * grid[2] + k, side_blocks - 1)

        in_specs.append(pl.BlockSpec((None, rows_blk, c), lambda i, j, k: (src_layer, side_block(i, j, k), 0)))
        args.append(src)
        out_specs.append(pl.BlockSpec((None, rows_blk, c), lambda i, j, k: (0, side_block(i, j, k), 0)))
        out_shape.append(jax.ShapeDtypeStruct((1, r, c), BF16))
        vmem += 2 * rows_blk * c * 6
        semantics = ("arbitrary", "arbitrary", "arbitrary")
    res = pl.pallas_call(
        functools.partial(_mm_kernel, n_a=n_a, kt=kt, epilogue=epilogue, group=group,
                          side_blocks=side_blocks, steps_jk=grid[1:],
                          row_scale_dim=0 if row_ssq is None else k_total, norm_out=next_gain is not None),
        grid=grid,
        in_specs=in_specs,
        out_specs=out_specs,
        out_shape=out_shape,
        compiler_params=pltpu.CompilerParams(
            dimension_semantics=semantics,
            vmem_limit_bytes=int(min(max(vmem + (6 << 20), 32 << 20), VMEM_LIMIT_CAP))),
        name=name,
    )(*args)
    return res if len(res) > 1 else res[0]


def _conv_kernel(ax_ref, ab_ref, ac_ref, buf_ref, w_ref, y_ref, new_ref, carry_ref, *, last_valid):
    t = pl.program_id(1)
    nt = pl.num_programs(1)
    rows = ax_ref.shape[0]
    u = ac_ref[...].astype(F32) * ax_ref[...].astype(F32)

    @pl.when(t == 0)
    def _():
        carry_ref[...] = buf_ref[0]

    prev = carry_ref[...]
    row = lax.broadcasted_iota(jnp.int32, u.shape, 0)
    u1 = jnp.where(row == 0, prev[1:2], pltpu.roll(u, 1, 0))
    u2 = jnp.where(row == 0, prev[0:1], jnp.where(row == 1, prev[1:2], pltpu.roll(u, 2, 0)))
    w = w_ref[...]
    conv = w[0:1] * u2 + w[1:2] * u1 + w[2:3] * u
    y_ref[...] = (ab_ref[...].astype(F32) * conv).astype(y_ref.dtype)
    carry_ref[...] = u[rows - 2:rows]

    @pl.when(t == nt - 1)
    def _():
        new_ref[0] = u[last_valid - 2:last_valid]


def short_conv(z, conv_buf, conv_w, *, batch, rows_per_seq, rows_blk, valid_rows, col_blocks):
    width = conv_w.shape[1]
    rows_blk = min(rows_blk, rows_per_seq)
    nt = rows_per_seq // rows_blk
    last_valid = valid_rows - (nt - 1) * rows_blk
    assert last_valid >= 2
    cx, cb, cc = col_blocks

    def zspec(cblk):
        return pl.BlockSpec((rows_blk, width), lambda b, t, cblk=cblk: (b * nt + t, cblk))

    return pl.pallas_call(
        functools.partial(_conv_kernel, last_valid=last_valid),
        grid=(batch, nt),
        in_specs=[zspec(cx), zspec(cb), zspec(cc),
                  pl.BlockSpec((1, 2, width), lambda b, t: (b, 0, 0)),
                  pl.BlockSpec((3, width), lambda b, t: (0, 0))],
        out_specs=[pl.BlockSpec((rows_blk, width), lambda b, t: (b * nt + t, 0)),
                   pl.BlockSpec((1, 2, width), lambda b, t: (b, 0, 0))],
        out_shape=[jax.ShapeDtypeStruct((batch * rows_per_seq, width), BF16),
                   jax.ShapeDtypeStruct((batch, 2, width), F32)],
        scratch_shapes=[pltpu.VMEM((2, width), F32)],
        compiler_params=_params(("parallel", "arbitrary"), rows_blk * width * 40),
        name="short_conv",
    )(z, z, z, conv_buf, conv_w)


def _cumsum_rows(x):
    rows = x.shape[0]
    row = lax.broadcasted_iota(jnp.int32, x.shape, 0)
    shift = 1
    while shift < rows:
        x = x + jnp.where(row >= shift, pltpu.roll(x, shift, 0), 0.0)
        shift *= 2
    return x


def _log_sigmoid(x):
    return jnp.minimum(x, 0.0) - jnp.log1p(jnp.exp(-jnp.abs(x)))


def _mlstm_kernel(q_ref, k_ref, v_ref, og_ref, g_ref, gbias_ref, gain_ref, c0_ref, n0_ref, m0_ref,
                  y_ref, c_ref, n_ref, m_ref, *, valid, heads, dk, dv):
    chunk = pl.program_id(1)
    L = CHUNK_ROWS
    lb = q_ref.shape[0]

    @pl.when(chunk == 0)
    def _():
        c_ref[...] = c0_ref[...]
        n_ref[...] = n0_ref[...]
        m_ref[...] = m0_ref[...]

    row = lax.broadcasted_iota(jnp.int32, (L, 1), 0)
    tri = lax.broadcasted_iota(jnp.int32, (L, L), 1) <= lax.broadcasted_iota(jnp.int32, (L, L), 0)
    row_ok = row < valid
    gates = _pad_rows(g_ref[...], L) + gbias_ref[...]
    ig_all = jnp.where(row_ok, gates, NEG)
    lf_all = jnp.where(row_ok, _log_sigmoid(gates), 0.0)
    b_all = _cumsum_rows(lf_all)
    ig_all_t = ig_all.T
    b_all_t = b_all.T
    scale = dk ** -0.5
    ys = []
    for h in range(heads):
        q = _pad_rows(q_ref[:, h * dk:(h + 1) * dk], L)
        k = _pad_rows(k_ref[:, h * dk:(h + 1) * dk], L)
        v = _pad_rows(v_ref[:, h * dv:(h + 1) * dv], L)
        ig_c = ig_all[:, h:h + 1]
        b_c = b_all[:, heads + h:heads + h + 1]
        ig_r = ig_all_t[h:h + 1, :]
        b_r = b_all_t[heads + h:heads + h + 1, :]
        c0 = c_ref[0, h]
        n0 = n_ref[0, h:h + 1, :]
        m0 = m_ref[0, :, h:h + 1]

        dmat = jnp.where(tri, b_c + (ig_r - b_r), NEG)
        inter = b_c + m0
        m_c = jnp.maximum(inter, jnp.max(dmat, axis=-1, keepdims=True))
        w_inter = jnp.exp(inter - m_c)
        qk = _nt_dot(q, k) * scale
        w_intra = jnp.exp(dmat - m_c) * qk
        num = w_inter * jnp.dot(q, c0.astype(BF16), preferred_element_type=F32)
        num = num + jnp.dot(w_intra.astype(BF16), v, preferred_element_type=F32)
        qn0 = jnp.sum(q.astype(F32) * n0, axis=-1, keepdims=True)
        den = w_inter * qn0 + jnp.sum(w_intra, axis=-1, keepdims=True)
        hid = num / jnp.maximum(jnp.abs(den), jnp.exp(-m_c))

        b_end = b_c[valid - 1:valid, :]
        m_end = m_c[valid - 1:valid, :]
        g_state = jnp.exp(b_end + m0 - m_end)
        g_tok = jnp.exp(b_end - b_c + ig_c - m_end) * scale
        kg = k.astype(F32) * g_tok
        c_ref[0, h] = g_state * c0 + jnp.dot(kg.T.astype(BF16), v, preferred_element_type=F32)
        n_ref[0, h:h + 1, :] = g_state * n0 + jnp.sum(kg, axis=0, keepdims=True)
        m_ref[0, :, h:h + 1] = m_end

        hn = _group_rms(hid, gain_ref[:, h * dv:(h + 1) * dv], dv)
        og = _pad_rows(og_ref[:, h * dv:(h + 1) * dv], L).astype(F32)
        ys.append(hn * jax.nn.sigmoid(og))
    y = jnp.concatenate(ys, axis=1)
    y_ref[...] = y[:lb].astype(y_ref.dtype)


def mlstm_mixer(z, gates, gate_bias, gain, c0, n0, m0, *, batch, rows_per_seq, valid_rows, cols):
    _, heads, dk, dv = c0.shape
    lb = min(CHUNK_ROWS, rows_per_seq)
    nc = rows_per_seq // lb
    assert rows_per_seq % lb == 0
    valid = valid_rows - (nc - 1) * lb
    assert (nc == 1 or valid == lb) and 1 <= valid <= lb
    q_off, k_off, v_off, o_off = cols
    wq, wv = heads * dk, heads * dv

    def zspec(width, off):
        assert off % width == 0
        return pl.BlockSpec((lb, width), lambda b, c, blk=off // width: (b * nc + c, blk))

    m0 = m0.reshape(batch, 1, heads)
    y, c, n, m = pl.pallas_call(
        functools.partial(_mlstm_kernel, valid=valid, heads=heads, dk=dk, dv=dv),
        grid=(batch, nc),
        in_specs=[zspec(wq, q_off), zspec(wq, k_off), zspec(wv, v_off), zspec(wv, o_off),
                  pl.BlockSpec((lb, V7X_LANES), lambda b, c: (b * nc + c, 0)),
                  pl.BlockSpec((1, V7X_LANES), lambda b, c: (0, 0)),
                  pl.BlockSpec((1, wv), lambda b, c: (0, 0)),
                  pl.BlockSpec((1, heads, dk, dv), lambda b, c: (b, 0, 0, 0)),
                  pl.BlockSpec((1, heads, dk), lambda b, c: (b, 0, 0)),
                  pl.BlockSpec((1, 1, heads), lambda b, c: (b, 0, 0))],
        out_specs=[pl.BlockSpec((lb, wv), lambda b, c: (b * nc + c, 0)),
                   pl.BlockSpec((1, heads, dk, dv), lambda b, c: (b, 0, 0, 0)),
                   pl.BlockSpec((1, heads, dk), lambda b, c: (b, 0, 0)),
                   pl.BlockSpec((1, 1, heads), lambda b, c: (b, 0, 0))],
        out_shape=[jax.ShapeDtypeStruct((batch * rows_per_seq, wv), BF16),
                   jax.ShapeDtypeStruct(c0.shape, F32),
                   jax.ShapeDtypeStruct(n0.shape, F32),
                   jax.ShapeDtypeStruct((batch, 1, heads), F32)],
        compiler_params=_params(("parallel", "arbitrary"), 16 << 20),
        name="mlstm",
    )(z, z, z, z, gates, gate_bias, gain.reshape(1, wv), c0, n0, m0)
    return y, c, n, m.reshape(batch, heads)


def _retention_kernel(q_ref, k_ref, v_ref, gd_ref, gain_ref, s0_ref, y_ref, s_ref, *, valid, heads, dk, dv):
    chunk = pl.program_id(1)
    L = CHUNK_ROWS
    lb = q_ref.shape[0]

    @pl.when(chunk == 0)
    def _():
        s_ref[...] = s0_ref[...]

    row = lax.broadcasted_iota(jnp.int32, (L, 1), 0)
    t_idx = lax.broadcasted_iota(jnp.int32, (L, L), 0)
    s_idx = lax.broadcasted_iota(jnp.int32, (L, L), 1)
    dist = (t_idx - s_idx).astype(F32)
    keep = (t_idx >= s_idx) & (s_idx < valid)
    rowf = row.astype(F32)
    scale = dk ** -0.5
    ys = []
    for h in range(heads):
        log_gamma = math.log1p(-(2.0 ** (-5.0 - h)))
        q = _pad_rows(q_ref[:, h * dk:(h + 1) * dk], L)
        k = _pad_rows(k_ref[:, h * dk:(h + 1) * dk], L)
        v = _pad_rows(v_ref[:, h * dv:(h + 1) * dv], L)
        s0 = s_ref[0, h]
        decay = jnp.where(keep, jnp.exp(jnp.maximum(dist, 0.0) * log_gamma), 0.0)
        smat = _nt_dot(q, k) * scale * decay
        inter = jnp.exp((rowf + 1.0) * log_gamma)
        o = jnp.dot(smat.astype(BF16), v, preferred_element_type=F32)
        o = o + inter * jnp.dot(q, s0.astype(BF16), preferred_element_type=F32)
        tail = jnp.where(row < valid, jnp.exp((valid - 1.0 - rowf) * log_gamma), 0.0) * scale
        kt = k.astype(F32) * tail
        s_ref[0, h] = math.exp(valid * log_gamma) * s0 + jnp.dot(kt.T.astype(BF16), v, preferred_element_type=F32)
        on = _group_rms(o, gain_ref[:, h * dv:(h + 1) * dv], dv)
        gd = _pad_rows(gd_ref[:, h * dv:(h + 1) * dv], L).astype(F32)
        ys.append(on * (gd * jax.nn.sigmoid(gd)))
    y = jnp.concatenate(ys, axis=1)
    y_ref[...] = y[:lb].astype(y_ref.dtype)


def retention_mixer(z, gain, s0, *, batch, rows_per_seq, valid_rows, cols):
    _, heads, dk, dv = s0.shape
    lb = min(CHUNK_ROWS, rows_per_seq)
    nc = rows_per_seq // lb
    valid = valid_rows - (nc - 1) * lb
    assert (nc == 1 or valid == lb) and 1 <= valid <= lb
    q_off, k_off, v_off, g_off = cols
    wq, wv = heads * dk, heads * dv

    def zspec(width, off):
        assert off % width == 0
        return pl.BlockSpec((lb, width), lambda b, c, blk=off // width: (b * nc + c, blk))

    return pl.pallas_call(
        functools.partial(_retention_kernel, valid=valid, heads=heads, dk=dk, dv=dv),
        grid=(batch, nc),
        in_specs=[zspec(wq, q_off), zspec(wq, k_off), zspec(wv, v_off), zspec(wv, g_off),
                  pl.BlockSpec((1, wv), lambda b, c: (0, 0)),
                  pl.BlockSpec((1, heads, dk, dv), lambda b, c: (b, 0, 0, 0))],
        out_specs=[pl.BlockSpec((lb, wv), lambda b, c: (b * nc + c, 0)),
                   pl.BlockSpec((1, heads, dk, dv), lambda b, c: (b, 0, 0, 0))],
        out_shape=[jax.ShapeDtypeStruct((batch * rows_per_seq, wv), BF16),
                   jax.ShapeDtypeStruct(s0.shape, F32)],
        compiler_params=_params(("parallel", "arbitrary"), 16 << 20),
        name="retention",
    )(z, z, z, z, gain.reshape(1, wv), s0)


def _store_cache_rows(ref, x, heads):
    rows, width = x.shape
    dv = width // heads
    tiles_per_head = dv // V7X_LANES
    step = heads * tiles_per_head
    for h in range(heads):
        for t in range(tiles_per_head):
            c0 = h * dv + t * V7X_LANES
            ref[pl.ds(t * heads + h, rows, stride=step), :] = x[:, c0:c0 + V7X_LANES]


def _load_cache_head(ref3, h, heads, tokens, dv):
    tiles_per_head = dv // V7X_LANES
    step = heads * tiles_per_head
    return jnp.concatenate([ref3[0, pl.ds(t * heads + h, tokens, stride=step), :]
                            for t in range(tiles_per_head)], axis=1)


def _cache_view(a, lead):
    tokens, heads, dv = a.shape[-3:]
    t = dv // V7X_LANES
    return (a.reshape(lead, tokens, heads, t, V7X_LANES).transpose(0, 1, 3, 2, 4)
            .reshape(lead, tokens * t * heads, V7X_LANES))


def _cache_unview(a, batch, tokens, heads, dv):
    t = dv // V7X_LANES
    return (a.reshape(batch, tokens, t, heads, V7X_LANES).transpose(0, 1, 3, 2, 4)
            .reshape(batch, tokens, heads, dv))


def _diff_prep_kernel(q_ref, k_ref, v_ref, gq_ref, gk_ref, qn_ref, kb_ref, kc_ref, vc_ref, *, dh, heads):
    qn_ref[...] = (_group_rms(q_ref[...].astype(F32), gq_ref[...], dh) * dh ** -0.5).astype(qn_ref.dtype)
    kn = _group_rms(k_ref[...].astype(F32), gk_ref[...], dh)
    kb_ref[...] = kn.astype(kb_ref.dtype)
    _store_cache_rows(kc_ref, kn, heads)
    _store_cache_rows(vc_ref, v_ref[...].astype(F32), heads)


def diff_prep(z, g_q, g_k, *, rows_blk, cols, width, heads):
    m = z.shape[0]
    dh = g_q.shape[0]
    rows_blk = min(rows_blk, m)
    q_off, k_off, v_off = cols
    il = width // V7X_LANES

    def zspec(off):
        assert off % width == 0
        return pl.BlockSpec((rows_blk, width), lambda i, blk=off // width: (i, blk))

    ospec = pl.BlockSpec((rows_blk, width), lambda i: (i, 0))
    cspec = pl.BlockSpec((rows_blk * il, V7X_LANES), lambda i: (i, 0))
    return pl.pallas_call(
        functools.partial(_diff_prep_kernel, dh=dh, heads=heads),
        grid=(m // rows_blk,),
        in_specs=[zspec(q_off), zspec(k_off), zspec(v_off),
                  pl.BlockSpec((1, dh), lambda i: (0, 0)), pl.BlockSpec((1, dh), lambda i: (0, 0))],
        out_specs=[ospec, ospec, cspec, cspec],
        out_shape=[jax.ShapeDtypeStruct((m, width), BF16),
                   jax.ShapeDtypeStruct((m, width), BF16),
                   jax.ShapeDtypeStruct((m * il, V7X_LANES), F32),
                   jax.ShapeDtypeStruct((m * il, V7X_LANES), F32)],
        compiler_params=_params(("parallel",), rows_blk * width * 48),
        name="diff_prep",
    )(z, z, z, g_q.reshape(1, dh), g_k.reshape(1, dh))


def _softmax_update(s, v, m_ref, l_ref, acc_ref):
    m_old = m_ref[...]
    m_new = jnp.maximum(m_old, jnp.max(s, axis=-1, keepdims=True))
    alpha = jnp.exp(m_old - m_new)
    p = jnp.exp(s - m_new)
    l_ref[...] = alpha * l_ref[...] + jnp.sum(p, axis=-1, keepdims=True)
    acc_ref[...] = alpha * acc_ref[...] + jnp.dot(p.astype(BF16), v, preferred_element_type=F32)
    m_ref[...] = m_new


def _diff_attn_prompt_kernel(qi_tab, ki_tab, slope_ref, lam_ref, q_ref, k_ref, v_ref, gain_ref, y_ref,
                             m_s, l_s, acc_s, *, bq, bk, dh, out_scale, rows_sub):
    h = pl.program_id(1)
    pair = pl.program_id(2)
    qi = qi_tab[pair]
    ki = ki_tab[pair]

    @pl.when(ki == 0)
    def _():
        m_s[...] = jnp.full(m_s.shape, NEG, F32)
        l_s[...] = jnp.zeros(l_s.shape, F32)
        acc_s[...] = jnp.zeros(acc_s.shape, F32)

    def block_update(diagonal):
        rel = lax.broadcasted_iota(jnp.int32, (1, bk), 1) + (ki * bk - qi * bq)
        bias = slope_ref[h] * rel.astype(F32)
        for r in range(bq // rows_sub):
            rows = pl.ds(r * rows_sub, rows_sub)
            nk = min(bk, (r + 1) * rows_sub) if diagonal else bk
            if diagonal:
                qrow = lax.broadcasted_iota(jnp.int32, (rows_sub, 1), 0) + r * rows_sub
                row_bias = jnp.where(rel[:, :nk] <= qrow, bias[:, :nk], NEG)
            else:
                row_bias = bias
            for j in range(2):
                s = _nt_dot(q_ref[rows, j * dh:(j + 1) * dh], k_ref[:nk, j * dh:(j + 1) * dh]) + row_bias
                _softmax_update(s, v_ref[:nk, :], m_s.at[j, rows], l_s.at[j, rows], acc_s.at[j, rows])

    @pl.when(ki < qi)
    def _():
        block_update(False)

    @pl.when(ki == qi)
    def _():
        block_update(True)
        o = acc_s[0] / l_s[0] - lam_ref[0] * (acc_s[1] / l_s[1])
        y_ref[...] = (_group_rms(o, gain_ref[...], o.shape[1]) * out_scale).astype(y_ref.dtype)


def diff_attention_prompt(qn, kn, z, slopes, lam, gain, *, batch, seq, heads, dh, dv, v_off, blk, out_scale):
    bq = bk = min(blk, seq)
    nq = seq // bq
    hw = 2 * dh
    assert hw == dv and v_off % dv == 0
    vblk = v_off // dv
    pairs = [(qi, ki) for qi in range(nq) for ki in range(qi + 1)]
    qi_tab = jnp.asarray([p[0] for p in pairs], jnp.int32)
    ki_tab = jnp.asarray([p[1] for p in pairs], jnp.int32)

    grid_spec = pltpu.PrefetchScalarGridSpec(
        num_scalar_prefetch=2,
        grid=(batch, heads, len(pairs)),
        in_specs=[pl.BlockSpec(memory_space=pltpu.SMEM),
                  pl.BlockSpec(memory_space=pltpu.SMEM),
                  pl.BlockSpec((bq, hw), lambda b, h, p, qt, kt: (b * nq + qt[p], h)),
                  pl.BlockSpec((bk, hw), lambda b, h, p, qt, kt: (b * nq + kt[p], h)),
                  pl.BlockSpec((bk, dv), lambda b, h, p, qt, kt: (b * nq + kt[p], vblk + h)),
                  pl.BlockSpec((1, dv), lambda b, h, p, qt, kt: (0, h))],
        out_specs=pl.BlockSpec((bq, dv), lambda b, h, p, qt, kt: (b * nq + qt[p], h)),
        scratch_shapes=[pltpu.VMEM((2, bq, 1), F32), pltpu.VMEM((2, bq, 1), F32), pltpu.VMEM((2, bq, dv), F32)],
    )
    return pl.pallas_call(
        functools.partial(_diff_attn_prompt_kernel, bq=bq, bk=bk, dh=dh, out_scale=out_scale,
                          rows_sub=min(256, bq)),
        grid_spec=grid_spec,
        out_shape=jax.ShapeDtypeStruct((batch * seq, heads * dv), BF16),
        compiler_params=_params(("parallel", "parallel", "arbitrary"), 24 << 20),
        name="diff_attn_prompt",
    )(qi_tab, ki_tab, slopes, lam, qn, kn, z, gain.reshape(1, heads * dv))


def _diff_attn_decode_kernel(pt_ref, slope_ref, lam_ref, q_ref, kn_ref, vn_ref, gain_ref, *rest,
                             pages, page, past, valid, heads, dh, dv, out_scale):
    k_refs = rest[:pages]
    v_refs = rest[pages:2 * pages]
    y_ref, m_s, l_s, acc_s = rest[2 * pages:]
    step = pl.program_id(1)
    nsteps = pl.num_programs(1)
    rows = q_ref.shape[0]
    hw = 2 * dh

    @pl.when(step == 0)
    def _():
        m_s[...] = jnp.full(m_s.shape, NEG, F32)
        l_s[...] = jnp.zeros(l_s.shape, F32)
        acc_s[...] = jnp.zeros(acc_s.shape, F32)

    def stacked_queries(h):
        qh = q_ref[:, h * hw:(h + 1) * hw]
        zero = jnp.zeros((rows, dh), qh.dtype)
        return jnp.concatenate([jnp.concatenate([qh[:, :dh], zero], axis=1),
                                jnp.concatenate([zero, qh[:, dh:]], axis=1)], axis=0)

    nkeys = pages * page
    rel = lax.broadcasted_iota(jnp.int32, (1, nkeys), 1) + (step * nkeys - past)
    for h in range(heads):
        qs = stacked_queries(h)
        k = jnp.concatenate([_load_cache_head(r, h, heads, page, hw) for r in k_refs], axis=0).astype(BF16)
        v = jnp.concatenate([_load_cache_head(r, h, heads, page, dv) for r in v_refs], axis=0).astype(BF16)
        s = _nt_dot(qs, k) + slope_ref[h] * rel.astype(F32)
        _softmax_update(s, v, m_s.at[h], l_s.at[h], acc_s.at[h])

    @pl.when(step == nsteps - 1)
    def _():
        nk = V7X_LANES
        key = lax.broadcasted_iota(jnp.int32, (1, nk), 1)
        tok = lax.broadcasted_iota(jnp.int32, (2 * rows, 1), 0) & (rows - 1)
        ok = (key <= tok) & (key < valid)
        for h in range(heads):
            qs = stacked_queries(h)
            k = _pad_rows(kn_ref[:, h * hw:(h + 1) * hw], nk)
            v = _pad_rows(vn_ref[:, h * dv:(h + 1) * dv], nk)
            s = _nt_dot(qs, k) + slope_ref[h] * key.astype(F32)
            s = jnp.where(ok, s, NEG)
            _softmax_update(s, v, m_s.at[h], l_s.at[h], acc_s.at[h])
            o = acc_s[h] / l_s[h]
            o = o[:rows] - lam_ref[0] * o[rows:]
            y_ref[:, h * dv:(h + 1) * dv] = (
                _group_rms(o, gain_ref[:, h * dv:(h + 1) * dv], dv) * out_scale).astype(y_ref.dtype)


def diff_attention_decode(qn, kn, z, cache_k, cache_v, page_table, layer, slopes, lam, gain,
                          *, batch, rows_per_seq, valid_rows, heads, dh, dv, v_off, out_scale):
    depth, n_pool, page = cache_k.shape[:3]
    n_pages = page_table.shape[1]
    pages = PAGES_PER_STEP
    assert n_pages % pages == 0
    nsteps = n_pages // pages
    width = heads * dv
    ck = _cache_view(cache_k, depth * n_pool)
    cv = _cache_view(cache_v, depth * n_pool)
    base = layer * n_pool
    page_rows = page * width // V7X_LANES

    def page_spec(p):
        return pl.BlockSpec((1, page_rows, V7X_LANES),
                            lambda b, s, pt, p=p: (base + pt[b * n_pages + s * pages + p], 0, 0))

    row_spec = pl.BlockSpec((rows_per_seq, width), lambda b, s, pt: (b, 0))
    grid_spec = pltpu.PrefetchScalarGridSpec(
        num_scalar_prefetch=1,
        grid=(batch, nsteps),
        in_specs=[pl.BlockSpec(memory_space=pltpu.SMEM),
                  pl.BlockSpec(memory_space=pltpu.SMEM),
                  row_spec, row_spec,
                  pl.BlockSpec((rows_per_seq, width), lambda b, s, pt: (b, v_off // width)),
                  pl.BlockSpec((1, width), lambda b, s, pt: (0, 0))]
                 + [page_spec(p) for p in range(pages)] + [page_spec(p) for p in range(pages)],
        out_specs=row_spec,
        scratch_shapes=[pltpu.VMEM((heads, 2 * rows_per_seq, 1), F32),
                        pltpu.VMEM((heads, 2 * rows_per_seq, 1), F32),
                        pltpu.VMEM((heads, 2 * rows_per_seq, dv), F32)],
    )
    assert v_off % width == 0
    return pl.pallas_call(
        functools.partial(_diff_attn_decode_kernel, pages=pages, page=page, past=n_pages * page,
                          valid=valid_rows, heads=heads, dh=dh, dv=dv, out_scale=out_scale),
        grid_spec=grid_spec,
        out_shape=jax.ShapeDtypeStruct((batch * rows_per_seq, width), BF16),
        compiler_params=_params(("parallel", "arbitrary"), 4 * pages * page * width * 4 + (8 << 20)),
        name="diff_attn_decode",
    )(page_table.reshape(-1), slopes, lam, qn, kn, z, gain.reshape(1, width),
      *([ck] * pages), *([cv] * pages))


def _cross_attn_kernel(q_ref, mk_ref, mv_ref, o_ref, *, heads, dh, n_mem, cache_layout):
    outs = []
    for h in range(heads):
        q = q_ref[:, h * dh:(h + 1) * dh]
        if cache_layout:
            mk = _load_cache_head(mk_ref, h, heads, n_mem, dh).astype(BF16)
            mv = _load_cache_head(mv_ref, h, heads, n_mem, dh).astype(BF16)
        else:
            mk = mk_ref[:, h * dh:(h + 1) * dh].astype(BF16)
            mv = mv_ref[:, h * dh:(h + 1) * dh].astype(BF16)
        s = _nt_dot(q, mk) * dh ** -0.5
        p = jnp.exp(s - jnp.max(s, axis=-1, keepdims=True))
        o = jnp.dot(p.astype(BF16), mv, preferred_element_type=F32)
        outs.append(o / jnp.sum(p, axis=-1, keepdims=True))
    o_ref[...] = jnp.concatenate(outs, axis=1).astype(o_ref.dtype)


def cross_attention(qx, mk, mv, *, batch, rows_per_seq, n_mem, heads, dh, rows_blk, cache_layer=None):
    width = heads * dh
    rows_blk = min(rows_blk, rows_per_seq)
    nq = rows_per_seq // rows_blk
    if cache_layer is None:
        mem_spec = pl.BlockSpec((n_mem, width), lambda b, i: (b, 0))
    else:
        lead = mk.shape[0] * mk.shape[1]
        mk, mv = _cache_view(mk, lead), _cache_view(mv, lead)
        mem_spec = pl.BlockSpec((1, n_mem * width // V7X_LANES, V7X_LANES),
                                lambda b, i: (cache_layer * batch + b, 0, 0))
    return pl.pallas_call(
        functools.partial(_cross_attn_kernel, heads=heads, dh=dh, n_mem=n_mem,
                          cache_layout=cache_layer is not None),
        grid=(batch, nq),
        in_specs=[pl.BlockSpec((rows_blk, width), lambda b, i: (b * nq + i, 0)), mem_spec, mem_spec],
        out_specs=pl.BlockSpec((rows_blk, width), lambda b, i: (b * nq + i, 0)),
        out_shape=jax.ShapeDtypeStruct((batch * rows_per_seq, width), BF16),
        compiler_params=_params(("parallel", "parallel"), 24 << 20),
        name="cross_attn",
    )(qx, mk, mv)


def kernel(x_prompt, x_sample, cache_diff_k, cache_diff_v, cache_mem_k, cache_mem_v, state_conv, state_mlstm_C, state_mlstm_n, state_mlstm_m, state_ret, page_table, mem_prompt, norm_mix, w_in, conv_w, b_igate, b_fgate, g_mlstm, g_diff_q, g_diff_k, lam_q1, lam_k1, lam_q2, lam_k2, g_diff_out, g_ret, w_out, norm_xattn, norm_mem, w_xq, w_xk, w_xv, g_xq, g_xk, w_xo, norm_mlp, w_up, w_down):
    depth = w_in.shape[0]
    bp, seq, d_model = x_prompt.shape
    bs, dec = x_sample.shape[:2]
    conv_ch = conv_w.shape[2]
    _, _, hb, dkb, dvb = state_mlstm_C.shape
    _, _, hd, dkd, dvd = state_ret.shape
    hc, dvc = cache_diff_v.shape[3:]
    dhc = dvc // 2
    _, _, n_mem, hx, dhx = cache_mem_k.shape
    xw = hx * dhx
    assert seq % CHUNK_ROWS == 0 and dec <= SAMPLE_ROWS

    sizes = (conv_ch, conv_ch, conv_ch, hb * dkb, hb * dkb, hb * dvb, hb * dvb, hb, hb,
             2 * hc * dhc, 2 * hc * dhc, hc * dvc, hd * dkd, hd * dkd, hd * dvd, hd * dvd)
    names = ("a_x", "a_b", "a_c", "q_b", "k_b", "v_b", "o_b", "i_b", "f_b",
             "q_c", "k_c", "v_c", "q_d", "k_d", "v_d", "g_d")
    off, pos = {}, 0
    for nm, sz in zip(names, sizes):
        if nm not in ("i_b", "f_b"):
            off[nm] = pos
            pos += sz

    slopes = (2.0 ** (-8.0 * jnp.arange(1, hc + 1, dtype=F32) / hc)).astype(F32)

    x_p = x_prompt.reshape(bp * seq, d_model)
    x_s = jnp.pad(x_sample, ((0, 0), (0, SAMPLE_ROWS - dec), (0, 0))).reshape(bs * SAMPLE_ROWS, d_model)
    mem_rows = mem_prompt.reshape(bp * n_mem, d_model)

    groups = (
        dict(batch=bp, rows=seq, valid=seq, tm=1024, prompt=True),
        dict(batch=bs, rows=SAMPLE_ROWS, valid=dec, tm=bs * SAMPLE_ROWS, prompt=False),
    )
    xs = [x_p, x_s]
    outs = [dict(conv=[], C=[], n=[], m=[], k=[], v=[], S=[], mk=[], mv=[]) for _ in groups]

    gate_lo = sum(sizes[:7])
    w_main, w_gate = split_w_in(w_in, gate_lo, gate_lo + 2 * hb)
    w_out_b, w_xq_b, w_xk_b, w_xv_b, w_xo_b = (cast_bf16(w) for w in (w_out, w_xq, w_xk, w_xv, w_xo))

    for li in range(depth):
        lam_init = 0.8 - 0.6 * math.exp(-0.3 * li)
        lam = (jnp.exp(jnp.sum(lam_q1[li] * lam_k1[li])) - jnp.exp(jnp.sum(lam_q2[li] * lam_k2[li]))
               + lam_init).reshape(1).astype(F32)
        gate_bias = jnp.pad(jnp.concatenate([b_igate[li], b_fgate[li]]), (0, V7X_LANES - 2 * hb)).reshape(1, V7X_LANES)

        hm = rmsnorm_rows(mem_rows, norm_mem[li], 256)
        mk_p = matmul([hm], w_xk_b, li, tm=1024, tn=1024, out_dtype=F32,
                      epilogue="headnorm", extra=g_xk[li], group=dhx, name="mm_mem_k")
        mv_p = matmul([hm], w_xv_b, li, tm=1024, tn=1024, out_dtype=F32, name="mm_mem_v")

        for gi, g in enumerate(groups):
            x = xs[gi]
            batch, rows, valid, tm = g["batch"], g["rows"], g["valid"], g["tm"]
            if g["prompt"]:
                conv_buf = jnp.zeros((batch, 2, conv_ch), F32)
                c0 = jnp.zeros((batch, hb, dkb, dvb), F32)
                n0 = jnp.zeros((batch, hb, dkb), F32)
                m0 = jnp.zeros((batch, hb), F32)
                s0 = jnp.zeros((batch, hd, dkd, dvd), F32)
                mk, mv, cache_layer = mk_p, mv_p, None
            else:
                conv_buf, c0, n0, m0, s0 = (state_conv[li], state_mlstm_C[li], state_mlstm_n[li],
                                            state_mlstm_m[li], state_ret[li])
                mk, mv, cache_layer = cache_mem_k, cache_mem_v, li

            h = rmsnorm_rows(x, norm_mix[li], 512)
            if g["prompt"]:
                z, w_up_l = matmul([h], w_main, li, tm=tm, tn=1024, side=(w_up, li), name="mm_in")
            else:
                z = matmul([h], w_main, li, tm=tm, tn=1024, name="mm_in")
            gates = matmul([h], w_gate, li, tm=tm, tn=V7X_LANES, out_dtype=F32, name="mm_gate")
            y_a, conv_new = short_conv(z, conv_buf, conv_w[li], batch=batch, rows_per_seq=rows, rows_blk=512,
                                       valid_rows=valid,
                                       col_blocks=(off["a_x"] // conv_ch, off["a_b"] // conv_ch, off["a_c"] // conv_ch))
            y_b, c_new, n_new, m_new = mlstm_mixer(
                z, gates, gate_bias, g_mlstm[li], c0, n0, m0, batch=batch, rows_per_seq=rows, valid_rows=valid,
                cols=(off["q_b"], off["k_b"], off["v_b"], off["o_b"]))
            qn, kb, kc, vc = diff_prep(z, g_diff_q[li], g_diff_k[li], rows_blk=512,
                                       cols=(off["q_c"], off["k_c"], off["v_c"]), width=hc * dvc, heads=hc)
            if g["prompt"]:
                y_c = diff_attention_prompt(qn, kb, z, slopes, lam, g_diff_out[li], batch=batch, seq=rows,
                                            heads=hc, dh=dhc, dv=dvc, v_off=off["v_c"], blk=1024,
                                            out_scale=1.0 - lam_init)
            else:
                y_c = diff_attention_decode(qn, kb, z, cache_diff_k, cache_diff_v, page_table, li, slopes, lam,
                                            g_diff_out[li], batch=batch, rows_per_seq=rows, valid_rows=valid,
                                            heads=hc, dh=dhc, dv=dvc, v_off=off["v_c"], out_scale=1.0 - lam_init)
            y_d, s_new = retention_mixer(z, g_ret[li], s0, batch=batch, rows_per_seq=rows, valid_rows=valid,
                                         cols=(off["q_d"], off["k_d"], off["v_d"], off["g_d"]))
            x, xg2, ssq2 = matmul([y_a, y_b, y_c, y_d], w_out_b, li, tm=tm, tn=512, out_dtype=F32,
                                  epilogue="residual", extra=x, next_gain=norm_xattn[li], name="mm_out")

            qx = matmul([xg2], w_xq_b, li, tm=tm, tn=1024, epilogue="headnorm", extra=g_xq[li], group=dhx,
                        row_ssq=ssq2, name="mm_xq")
            ox = cross_attention(qx, mk, mv, batch=batch, rows_per_seq=rows, n_mem=n_mem, heads=hx, dh=dhx,
                                 rows_blk=512, cache_layer=cache_layer)
            x, xg3, ssq3 = matmul([ox], w_xo_b, li, tm=tm, tn=1024, out_dtype=F32, epilogue="residual", extra=x,
                                  next_gain=norm_mlp[li], name="mm_xo")

            if g["prompt"]:
                hid, w_down_l = matmul([xg3], w_up_l, 0, tm=tm, tn=1024, epilogue="relu2", row_ssq=ssq3,
                                       side=(w_down, li), name="mm_up")
            else:
                hid = matmul([xg3], w_up_l, 0, tm=tm, tn=1024, epilogue="relu2", row_ssq=ssq3, name="mm_up")
            x = matmul([hid], w_down_l, 0, tm=tm, tn=1024, tk=4096, out_dtype=F32,
                       epilogue="residual", extra=x, name="mm_down")

            xs[gi] = x
            o = outs[gi]
            o["conv"].append(conv_new); o["C"].append(c_new); o["n"].append(n_new); o["m"].append(m_new)
            o["k"].append(kc); o["v"].append(vc); o["S"].append(s_new)
            if g["prompt"]:
                o["mk"].append(mk_p); o["mv"].append(mv_p)

    op, os_ = outs
    y_prompt = xs[0].reshape(bp, seq, d_model)
    y_sample = xs[1].reshape(bs, SAMPLE_ROWS, d_model)[:, :dec]

    def cache_rows(parts, batch, rows, valid):
        a = jnp.stack([_cache_unview(p, batch, rows, hc, dvc) for p in parts])
        return a if rows == valid else a[:, :, :valid]

    return (y_prompt, y_sample,
            jnp.stack(op["conv"]), jnp.stack(op["C"]), jnp.stack(op["n"]), jnp.stack(op["m"]),
            cache_rows(op["k"], bp, seq, seq), cache_rows(op["v"], bp, seq, seq), jnp.stack(op["S"]),
            jnp.stack(op["mk"]).reshape(depth, bp, n_mem, hx, dhx),
            jnp.stack(op["mv"]).reshape(depth, bp, n_mem, hx, dhx),
            jnp.stack(os_["conv"]), jnp.stack(os_["C"]), jnp.stack(os_["n"]), jnp.stack(os_["m"]),
            cache_rows(os_["k"], bs, SAMPLE_ROWS, dec), cache_rows(os_["v"], bs, SAMPLE_ROWS, dec),
            jnp.stack(os_["S"]))
```

```python
import functools
import math

import jax
import jax.numpy as jnp
from jax import lax
from jax.experimental import pallas as pl
from jax.experimental.pallas import tpu as pltpu

V7X_LANES = 128
V7X_VMEM_BYTES = 64 * 1024 * 1024
VMEM_LIMIT_CAP = V7X_VMEM_BYTES - 6 * 1024 * 1024

EPS = 1e-6
NEG = -1e30
CHUNK_ROWS = 256
SAMPLE_ROWS = 16
PAGES_PER_STEP = 16

BF16 = jnp.bfloat16
F32 = jnp.float32


def _params(semantics, vmem_estimate):
    limit = int(min(max(vmem_estimate * 5 // 4 + (4 << 20), 32 << 20), VMEM_LIMIT_CAP))
    return pltpu.CompilerParams(dimension_semantics=semantics, vmem_limit_bytes=limit)


def _nt_dot(a, b):
    return lax.dot_general(a, b, (((1,), (1,)), ((), ())), preferred_element_type=F32)


def _pad_rows(x, rows):
    if x.shape[0] == rows:
        return x
    return jnp.concatenate([x, jnp.zeros((rows - x.shape[0],) + x.shape[1:], x.dtype)], axis=0)


def _group_rms(x, gain, width):
    parts = []
    for g in range(x.shape[1] // width):
        blk = x[:, g * width:(g + 1) * width]
        ms = jnp.mean(blk * blk, axis=-1, keepdims=True)
        parts.append(blk * lax.rsqrt(ms + EPS) * gain)
    return parts[0] if len(parts) == 1 else jnp.concatenate(parts, axis=1)


def _rmsnorm_kernel(x_ref, g_ref, o_ref):
    x = x_ref[...]
    ms = jnp.mean(x * x, axis=-1, keepdims=True)
    o_ref[...] = (x * lax.rsqrt(ms + EPS) * g_ref[...]).astype(o_ref.dtype)


def rmsnorm_rows(x, g, rows_blk):
    m, d = x.shape
    rows_blk = min(rows_blk, m)
    return pl.pallas_call(
        _rmsnorm_kernel,
        grid=(m // rows_blk,),
        in_specs=[pl.BlockSpec((rows_blk, d), lambda i: (i, 0)),
                  pl.BlockSpec((1, d), lambda i: (0, 0))],
        out_specs=pl.BlockSpec((rows_blk, d), lambda i: (i, 0)),
        out_shape=jax.ShapeDtypeStruct((m, d), BF16),
        compiler_params=_params(("parallel",), rows_blk * d * 12),
        name="rmsnorm",
    )(x, g.reshape(1, d))


def _cast_kernel(x_ref, o_ref):
    o_ref[...] = x_ref[...].astype(o_ref.dtype)


def cast_bf16(w, rows_blk=256, cols_blk=4096):
    d, k, n = w.shape
    tr, tc = min(rows_blk, k), min(cols_blk, n)
    assert k % tr == 0 and n % tc == 0
    spec = pl.BlockSpec((None, tr, tc), lambda l, i, j: (l, i, j))
    return pl.pallas_call(
        _cast_kernel,
        grid=(d, k // tr, n // tc),
        in_specs=[spec], out_specs=spec,
        out_shape=jax.ShapeDtypeStruct(w.shape, BF16),
        compiler_params=_params(("parallel", "parallel", "parallel"), tr * tc * 16),
        name="cast_bf16",
    )(w)


def _transpose_cast_kernel(x_ref, o_ref):
    o_ref[...] = x_ref[0].T.astype(o_ref.dtype)


def split_w_in(w_in, g0, g1, tn=1024, tk=1024):
    d, k, n = w_in.shape
    glen = g1 - g0
    n_main = n - glen
    assert g0 % tn == 0 and n_main % tn == 0 and k % tk == 0 and glen % 8 == 0 and glen <= V7X_LANES
    w_t = jnp.swapaxes(w_in, 1, 2)

    def src_index(l, j, kk):
        row = j * tn + jnp.where(j * tn >= g0, glen, 0)
        return (l, pl.multiple_of(row, 8), pl.multiple_of(kk * tk, V7X_LANES))

    main = pl.pallas_call(
        _transpose_cast_kernel,
        grid=(d, n_main // tn, k // tk),
        in_specs=[pl.BlockSpec((pl.Element(1), pl.Element(tn), pl.Element(tk)), src_index)],
        out_specs=pl.BlockSpec((None, tk, tn), lambda l, j, kk: (l, kk, j)),
        out_shape=jax.ShapeDtypeStruct((d, k, n_main), BF16),
        compiler_params=_params(("parallel", "parallel", "parallel"), tn * tk * 24),
        name="split_w_in",
    )(w_t)
    gate = jnp.pad(w_in[:, :, g0:g1], ((0, 0), (0, 0), (0, V7X_LANES - glen))).astype(BF16)
    return main, gate


def _mm_kernel(*refs, n_a, kt, epilogue, group, side_blocks, steps_jk, row_scale_dim, norm_out):
    a_refs = refs[:n_a]
    w_refs = refs[n_a:2 * n_a]
    pos = 2 * n_a
    extra_ref = ssq_in = gain_next = side_in = None
    if epilogue in ("residual", "headnorm"):
        extra_ref = refs[pos]
        pos += 1
    if row_scale_dim:
        ssq_in = refs[pos]
        pos += 1
    if norm_out:
        gain_next = refs[pos]
        pos += 1
    if side_blocks:
        side_in = refs[pos]
        pos += 1
    o_ref = refs[pos]
    pos += 1
    if norm_out:
        xg_ref, ssq_out = refs[pos], refs[pos + 1]
        pos += 2
    side_out = refs[pos] if side_blocks else None

    if side_blocks:
        step = (pl.program_id(0) * steps_jk[0] + pl.program_id(1)) * steps_jk[1] + pl.program_id(2)

        @pl.when(step < side_blocks)
        def _():
            side_out[...] = side_in[...].astype(side_out.dtype)

    part = jnp.dot(a_refs[0][...], w_refs[0][...], preferred_element_type=F32)
    for a_ref, w_ref in zip(a_refs[1:], w_refs[1:]):
        part = part + jnp.dot(a_ref[...], w_ref[...], preferred_element_type=F32)

    if row_scale_dim:
        ssq = ssq_in[...]
        total = ssq[:, 0:1]
        for g in range(1, ssq.shape[1] // V7X_LANES):
            total = total + ssq[:, g * V7X_LANES:g * V7X_LANES + 1]
        part = part * lax.rsqrt(total * (1.0 / row_scale_dim) + EPS)

    def emit_norm_input(x_new):
        xg_ref[...] = (x_new * gain_next[...]).astype(xg_ref.dtype)
        ssq_out[...] = jnp.broadcast_to(jnp.sum(x_new * x_new, axis=-1, keepdims=True), ssq_out.shape)

    if kt > 1:
        k = pl.program_id(2)

        @pl.when(k == 0)
        def _():
            o_ref[...] = extra_ref[...] + part

        @pl.when(k > 0)
        def _():
            o_ref[...] += part
    elif epilogue == "residual":
        x_new = extra_ref[...] + part
        o_ref[...] = x_new
        if norm_out:
            emit_norm_input(x_new)
    elif epilogue == "relu2":
        o_ref[...] = jnp.square(jnp.maximum(part, 0.0)).astype(o_ref.dtype)
    elif epilogue == "headnorm":
        o_ref[...] = _group_rms(part, extra_ref[...], group).astype(o_ref.dtype)
    else:
        o_ref[...] = part.astype(o_ref.dtype)


def matmul(a_list, w, layer, *, tm, tn, tk=None, out_dtype=BF16, epilogue="cast", extra=None, group=None,
           side=None, row_ssq=None, next_gain=None, name="mm"):
    n_a = len(a_list)
    m, kc = a_list[0].shape
    _, k_total, n = w.shape
    assert kc * n_a == k_total
    tm = min(tm, m)
    tn = min(tn, n)
    tk = kc if tk is None else min(tk, kc)
    assert m % tm == 0 and n % tn == 0 and kc % tk == 0
    kt = kc // tk
    assert kt == 1 or (n_a == 1 and epilogue == "residual" and out_dtype == F32)
    kblocks_per_chunk = kc // tk
    grid = (m // tm, n // tn, kt)

    in_specs = []
    for c in range(n_a):
        in_specs.append(pl.BlockSpec((tm, tk), lambda i, j, k: (i, k)))
    for c in range(n_a):
        in_specs.append(pl.BlockSpec((None, tk, tn), lambda i, j, k, c=c: (layer, c * kblocks_per_chunk + k, j)))
    args = list(a_list) + [w] * n_a
    out_bytes = jnp.dtype(out_dtype).itemsize
    vmem = 2 * n_a * (tm * tk + tk * tn) * 2 + 2 * tm * tn * out_bytes + tm * tn * 4
    if epilogue == "residual":
        in_specs.append(pl.BlockSpec((tm, tn), lambda i, j, k: (i, j)))
        args.append(extra)
        vmem += 2 * tm * tn * 4
    elif epilogue == "headnorm":
        assert tn % group == 0
        in_specs.append(pl.BlockSpec((1, group), lambda i, j, k: (0, 0)))
        args.append(extra.reshape(1, group))
    if row_ssq is not None:
        assert kt == 1 and n_a == 1 and row_ssq.shape[0] == m
        in_specs.append(pl.BlockSpec((tm, row_ssq.shape[1]), lambda i, j, k: (i, 0)))
        args.append(row_ssq)
        vmem += 2 * tm * row_ssq.shape[1] * 4
    out_specs = [pl.BlockSpec((tm, tn), lambda i, j, k: (i, j))]
    out_shape = [jax.ShapeDtypeStruct((m, n), out_dtype)]
    if next_gain is not None:
        assert kt == 1 and epilogue == "residual"
        in_specs.append(pl.BlockSpec((1, tn), lambda i, j, k: (0, j)))
        args.append(next_gain.reshape(1, n))
        out_specs += [pl.BlockSpec((tm, tn), lambda i, j, k: (i, j)),
                      pl.BlockSpec((tm, V7X_LANES), lambda i, j, k: (i, j))]
        out_shape += [jax.ShapeDtypeStruct((m, n), BF16),
                      jax.ShapeDtypeStruct((m, (n // tn) * V7X_LANES), F32)]
        vmem += 2 * tm * tn * 2 + 2 * tm * V7X_LANES * 4
    side_blocks = 0
    semantics = ("parallel", "parallel", "arbitrary")
    if side is not None:
        src, src_layer = side
        _, r, c = src.shape
        steps = grid[0] * grid[1] * grid[2]
        side_blocks = 1 << (steps.bit_length() - 1)
        rows_blk = r // side_blocks
        assert rows_blk * side_blocks == r and rows_blk % 16 == 0

        def side_block(i, j, k):
            return jnp.minimum((i * grid[1] + j) * grid[2] + k, side_blocks - 1)

        in_specs.append(pl.BlockSpec((None, rows_blk, c), lambda i, j, k: (src_layer, side_block(i, j, k), 0)))
        args.append(src)
        out_specs.append(pl.BlockSpec((None, rows_blk, c), lambda i, j, k: (0, side_block(i, j, k), 0)))
        out_shape.append(jax.ShapeDtypeStruct((1, r, c), BF16))
        vmem += 2 * rows_blk * c * 6
        semantics = ("arbitrary", "arbitrary", "arbitrary")
    res = pl.pallas_call(
        functools.partial(_mm_kernel, n_a=n_a, kt=kt, epilogue=epilogue, group=group,
                          side_blocks=side_blocks, steps_jk=grid[1:],
                          row_scale_dim=0 if row_ssq is None else k_total, norm_out=next_gain is not None),
        grid=grid,
        in_specs=in_specs,
        out_specs=out_specs,
        out_shape=out_shape,
        compiler_params=pltpu.CompilerParams(
            dimension_semantics=semantics,
            vmem_limit_bytes=int(min(max(vmem + (6 << 20), 32 << 20), VMEM_LIMIT_CAP))),
        name=name,
    )(*args)
    return res if len(res) > 1 else res[0]


def _conv_kernel(ax_ref, ab_ref, ac_ref, buf_ref, w_ref, y_ref, new_ref, carry_ref, *, last_valid):
    t = pl.program_id(1)
    nt = pl.num_programs(1)
    rows = ax_ref.shape[0]
    u = ac_ref[...].astype(F32) * ax_ref[...].astype(F32)

    @pl.when(t == 0)
    def _():
        carry_ref[...] = buf_ref[0]

    prev = carry_ref[...]
    row = lax.broadcasted_iota(jnp.int32, u.shape, 0)
    u1 = jnp.where(row == 0, prev[1:2], pltpu.roll(u, 1, 0))
    u2 = jnp.where(row == 0, prev[0:1], jnp.where(row == 1, prev[1:2], pltpu.roll(u, 2, 0)))
    w = w_ref[...]
    conv = w[0:1] * u2 + w[1:2] * u1 + w[2:3] * u
    y_ref[...] = (ab_ref[...].astype(F32) * conv).astype(y_ref.dtype)
    carry_ref[...] = u[rows - 2:rows]

    @pl.when(t == nt - 1)
    def _():
        new_ref[0] = u[last_valid - 2:last_valid]


def short_conv(z, conv_buf, conv_w, *, batch, rows_per_seq, rows_blk, valid_rows, col_blocks):
    width = conv_w.shape[1]
    rows_blk = min(rows_blk, rows_per_seq)
    nt = rows_per_seq // rows_blk
    last_valid = valid_rows - (nt - 1) * rows_blk
    assert last_valid >= 2
    cx, cb, cc = col_blocks

    def zspec(cblk):
        return pl.BlockSpec((rows_blk, width), lambda b, t, cblk=cblk: (b * nt + t, cblk))

    return pl.pallas_call(
        functools.partial(_conv_kernel, last_valid=last_valid),
        grid=(batch, nt),
        in_specs=[zspec(cx), zspec(cb), zspec(cc),
                  pl.BlockSpec((1, 2, width), lambda b, t: (b, 0, 0)),
                  pl.BlockSpec((3, width), lambda b, t: (0, 0))],
        out_specs=[pl.BlockSpec((rows_blk, width), lambda b, t: (b * nt + t, 0)),
                   pl.BlockSpec((1, 2, width), lambda b, t: (b, 0, 0))],
        out_shape=[jax.ShapeDtypeStruct((batch * rows_per_seq, width), BF16),
                   jax.ShapeDtypeStruct((batch, 2, width), F32)],
        scratch_shapes=[pltpu.VMEM((2, width), F32)],
        compiler_params=_params(("parallel", "arbitrary"), rows_blk * width * 40),
        name="short_conv",
    )(z, z, z, conv_buf, conv_w)


def _cumsum_rows(x):
    rows = x.shape[0]
    row = lax.broadcasted_iota(jnp.int32, x.shape, 0)
    shift = 1
    while shift < rows:
        x = x + jnp.where(row >= shift, pltpu.roll(x, shift, 0), 0.0)
        shift *= 2
    return x


def _log_sigmoid(x):
    return jnp.minimum(x, 0.0) - jnp.log1p(jnp.exp(-jnp.abs(x)))


def _mlstm_kernel(q_ref, k_ref, v_ref, og_ref, g_ref, gbias_ref, gain_ref, c0_ref, n0_ref, m0_ref,
                  y_ref, c_ref, n_ref, m_ref, *, valid, heads, dk, dv):
    chunk = pl.program_id(1)
    L = CHUNK_ROWS
    lb = q_ref.shape[0]

    @pl.when(chunk == 0)
    def _():
        c_ref[...] = c0_ref[...]
        n_ref[...] = n0_ref[...]
        m_ref[...] = m0_ref[...]

    row = lax.broadcasted_iota(jnp.int32, (L, 1), 0)
    tri = lax.broadcasted_iota(jnp.int32, (L, L), 1) <= lax.broadcasted_iota(jnp.int32, (L, L), 0)
    row_ok = row < valid
    gates = _pad_rows(g_ref[...], L) + gbias_ref[...]
    ig_all = jnp.where(row_ok, gates, NEG)
    lf_all = jnp.where(row_ok, _log_sigmoid(gates), 0.0)
    b_all = _cumsum_rows(lf_all)
    ig_all_t = ig_all.T
    b_all_t = b_all.T
    scale = dk ** -0.5
    ys = []
    for h in range(heads):
        q = _pad_rows(q_ref[:, h * dk:(h + 1) * dk], L)
        k = _pad_rows(k_ref[:, h * dk:(h + 1) * dk], L)
        v = _pad_rows(v_ref[:, h * dv:(h + 1) * dv], L)
        ig_c = ig_all[:, h:h + 1]
        b_c = b_all[:, heads + h:heads + h + 1]
        ig_r = ig_all_t[h:h + 1, :]
        b_r = b_all_t[heads + h:heads + h + 1, :]
        c0 = c_ref[0, h]
        n0 = n_ref[0, h:h + 1, :]
        m0 = m_ref[0, :, h:h + 1]

        dmat = jnp.where(tri, b_c + (ig_r - b_r), NEG)
        inter = b_c + m0
        m_c = jnp.maximum(inter, jnp.max(dmat, axis=-1, keepdims=True))
        w_inter = jnp.exp(inter - m_c)
        qk = _nt_dot(q, k) * scale
        w_intra = jnp.exp(dmat - m_c) * qk
        num = w_inter * jnp.dot(q, c0.astype(BF16), preferred_element_type=F32)
        num = num + jnp.dot(w_intra.astype(BF16), v, preferred_element_type=F32)
        qn0 = jnp.sum(q.astype(F32) * n0, axis=-1, keepdims=True)
        den = w_inter * qn0 + jnp.sum(w_intra, axis=-1, keepdims=True)
        hid = num / jnp.maximum(jnp.abs(den), jnp.exp(-m_c))

        b_end = b_c[valid - 1:valid, :]
        m_end = m_c[valid - 1:valid, :]
        g_state = jnp.exp(b_end + m0 - m_end)
        g_tok = jnp.exp(b_end - b_c + ig_c - m_end) * scale
        kg = k.astype(F32) * g_tok
        c_ref[0, h] = g_state * c0 + jnp.dot(kg.T.astype(BF16), v, preferred_element_type=F32)
        n_ref[0, h:h + 1, :] = g_state * n0 + jnp.sum(kg, axis=0, keepdims=True)
        m_ref[0, :, h:h + 1] = m_end

        hn = _group_rms(hid, gain_ref[:, h * dv:(h + 1) * dv], dv)
        og = _pad_rows(og_ref[:, h * dv:(h + 1) * dv], L).astype(F32)
        ys.append(hn * jax.nn.sigmoid(og))
    y = jnp.concatenate(ys, axis=1)
    y_ref[...] = y[:lb].astype(y_ref.dtype)


def mlstm_mixer(z, gates, gate_bias, gain, c0, n0, m0, *, batch, rows_per_seq, valid_rows, cols):
    _, heads, dk, dv = c0.shape
    lb = min(CHUNK_ROWS, rows_per_seq)
    nc = rows_per_seq // lb
    assert rows_per_seq % lb == 0
    valid = valid_rows - (nc - 1) * lb
    assert (nc == 1 or valid == lb) and 1 <= valid <= lb
    q_off, k_off, v_off, o_off = cols
    wq, wv = heads * dk, heads * dv

    def zspec(width, off):
        assert off % width == 0
        return pl.BlockSpec((lb, width), lambda b, c, blk=off // width: (b * nc + c, blk))

    m0 = m0.reshape(batch, 1, heads)
    y, c, n, m = pl.pallas_call(
        functools.partial(_mlstm_kernel, valid=valid, heads=heads, dk=dk, dv=dv),
        grid=(batch, nc),
        in_specs=[zspec(wq, q_off), zspec(wq, k_off), zspec(wv, v_off), zspec(wv, o_off),
                  pl.BlockSpec((lb, V7X_LANES), lambda b, c: (b * nc + c, 0)),
                  pl.BlockSpec((1, V7X_LANES), lambda b, c: (0, 0)),
                  pl.BlockSpec((1, wv), lambda b, c: (0, 0)),
                  pl.BlockSpec((1, heads, dk, dv), lambda b, c: (b, 0, 0, 0)),
                  pl.BlockSpec((1, heads, dk), lambda b, c: (b, 0, 0)),
                  pl.BlockSpec((1, 1, heads), lambda b, c: (b, 0, 0))],
        out_specs=[pl.BlockSpec((lb, wv), lambda b, c: (b * nc + c, 0)),
                   pl.BlockSpec((1, heads, dk, dv), lambda b, c: (b, 0, 0, 0)),
                   pl.BlockSpec((1, heads, dk), lambda b, c: (b, 0, 0)),
                   pl.BlockSpec((1, 1, heads), lambda b, c: (b, 0, 0))],
        out_shape=[jax.ShapeDtypeStruct((batch * rows_per_seq, wv), BF16),
                   jax.ShapeDtypeStruct(c0.shape, F32),
                   jax.ShapeDtypeStruct(n0.shape, F32),
                   jax.ShapeDtypeStruct((batch, 1, heads), F32)],
        compiler_params=_params(("parallel", "arbitrary"), 16 << 20),
        name="mlstm",
    )(z, z, z, z, gates, gate_bias, gain.reshape(1, wv), c0, n0, m0)
    return y, c, n, m.reshape(batch, heads)


def _retention_kernel(q_ref, k_ref, v_ref, gd_ref, gain_ref, s0_ref, y_ref, s_ref, *, valid, heads, dk, dv):
    chunk = pl.program_id(1)
    L = CHUNK_ROWS
    lb = q_ref.shape[0]

    @pl.when(chunk == 0)
    def _():
        s_ref[...] = s0_ref[...]

    row = lax.broadcasted_iota(jnp.int32, (L, 1), 0)
    t_idx = lax.broadcasted_iota(jnp.int32, (L, L), 0)
    s_idx = lax.broadcasted_iota(jnp.int32, (L, L), 1)
    dist = (t_idx - s_idx).astype(F32)
    keep = (t_idx >= s_idx) & (s_idx < valid)
    rowf = row.astype(F32)
    scale = dk ** -0.5
    ys = []
    for h in range(heads):
        log_gamma = math.log1p(-(2.0 ** (-5.0 - h)))
        q = _pad_rows(q_ref[:, h * dk:(h + 1) * dk], L)
        k = _pad_rows(k_ref[:, h * dk:(h + 1) * dk], L)
        v = _pad_rows(v_ref[:, h * dv:(h + 1) * dv], L)
        s0 = s_ref[0, h]
        decay = jnp.where(keep, jnp.exp(jnp.maximum(dist, 0.0) * log_gamma), 0.0)
        smat = _nt_dot(q, k) * scale * decay
        inter = jnp.exp((rowf + 1.0) * log_gamma)
        o = jnp.dot(smat.astype(BF16), v, preferred_element_type=F32)
        o = o + inter * jnp.dot(q, s0.astype(BF16), preferred_element_type=F32)
        tail = jnp.where(row < valid, jnp.exp((valid - 1.0 - rowf) * log_gamma), 0.0) * scale
        kt = k.astype(F32) * tail
        s_ref[0, h] = math.exp(valid * log_gamma) * s0 + jnp.dot(kt.T.astype(BF16), v, preferred_element_type=F32)
        on = _group_rms(o, gain_ref[:, h * dv:(h + 1) * dv], dv)
        gd = _pad_rows(gd_ref[:, h * dv:(h + 1) * dv], L).astype(F32)
        ys.append(on * (gd * jax.nn.sigmoid(gd)))
    y = jnp.concatenate(ys, axis=1)
    y_ref[...] = y[:lb].astype(y_ref.dtype)


def retention_mixer(z, gain, s0, *, batch, rows_per_seq, valid_rows, cols):
    _, heads, dk, dv = s0.shape
    lb = min(CHUNK_ROWS, rows_per_seq)
    nc = rows_per_seq // lb
    valid = valid_rows - (nc - 1) * lb
    assert (nc == 1 or valid == lb) and 1 <= valid <= lb
    q_off, k_off, v_off, g_off = cols
    wq, wv = heads * dk, heads * dv

    def zspec(width, off):
        assert off % width == 0
        return pl.BlockSpec((lb, width), lambda b, c, blk=off // width: (b * nc + c, blk))

    return pl.pallas_call(
        functools.partial(_retention_kernel, valid=valid, heads=heads, dk=dk, dv=dv),
        grid=(batch, nc),
        in_specs=[zspec(wq, q_off), zspec(wq, k_off), zspec(wv, v_off), zspec(wv, g_off),
                  pl.BlockSpec((1, wv), lambda b, c: (0, 0)),
                  pl.BlockSpec((1, heads, dk, dv), lambda b, c: (b, 0, 0, 0))],
        out_specs=[pl.BlockSpec((lb, wv), lambda b, c: (b * nc + c, 0)),
                   pl.BlockSpec((1, heads, dk, dv), lambda b, c: (b, 0, 0, 0))],
        out_shape=[jax.ShapeDtypeStruct((batch * rows_per_seq, wv), BF16),
                   jax.ShapeDtypeStruct(s0.shape, F32)],
        compiler_params=_params(("parallel", "arbitrary"), 16 << 20),
        name="retention",
    )(z, z, z, z, gain.reshape(1, wv), s0)


def _store_cache_rows(ref, x, heads):
    rows, width = x.shape
    dv = width // heads
    tiles_per_head = dv // V7X_LANES
    step = heads * tiles_per_head
    for h in range(heads):
        for t in range(tiles_per_head):
            c0 = h * dv + t * V7X_LANES
            ref[pl.ds(t * heads + h, rows, stride=step), :] = x[:, c0:c0 + V7X_LANES]


def _load_cache_head(ref3, h, heads, tokens, dv):
    tiles_per_head = dv // V7X_LANES
    step = heads * tiles_per_head
    return jnp.concatenate([ref3[0, pl.ds(t * heads + h, tokens, stride=step), :]
                            for t in range(tiles_per_head)], axis=1)


def _cache_view(a, lead):
    tokens, heads, dv = a.shape[-3:]
    t = dv // V7X_LANES
    return (a.reshape(lead, tokens, heads, t, V7X_LANES).transpose(0, 1, 3, 2, 4)
            .reshape(lead, tokens * t * heads, V7X_LANES))


def _cache_unview(a, batch, tokens, heads, dv):
    t = dv // V7X_LANES
    return (a.reshape(batch, tokens, t, heads, V7X_LANES).transpose(0, 1, 3, 2, 4)
            .reshape(batch, tokens, heads, dv))


def _diff_prep_kernel(q_ref, k_ref, v_ref, gq_ref, gk_ref, qn_ref, kb_ref, kc_ref, vc_ref, *, dh, heads):
    qn_ref[...] = (_group_rms(q_ref[...].astype(F32), gq_ref[...], dh) * dh ** -0.5).astype(qn_ref.dtype)
    kn = _group_rms(k_ref[...].astype(F32), gk_ref[...], dh)
    kb_ref[...] = kn.astype(kb_ref.dtype)
    _store_cache_rows(kc_ref, kn, heads)
    _store_cache_rows(vc_ref, v_ref[...].astype(F32), heads)


def diff_prep(z, g_q, g_k, *, rows_blk, cols, width, heads):
    m = z.shape[0]
    dh = g_q.shape[0]
    rows_blk = min(rows_blk, m)
    q_off, k_off, v_off = cols
    il = width // V7X_LANES

    def zspec(off):
        assert off % width == 0
        return pl.BlockSpec((rows_blk, width), lambda i, blk=off // width: (i, blk))

    ospec = pl.BlockSpec((rows_blk, width), lambda i: (i, 0))
    cspec = pl.BlockSpec((rows_blk * il, V7X_LANES), lambda i: (i, 0))
    return pl.pallas_call(
        functools.partial(_diff_prep_kernel, dh=dh, heads=heads),
        grid=(m // rows_blk,),
        in_specs=[zspec(q_off), zspec(k_off), zspec(v_off),
                  pl.BlockSpec((1, dh), lambda i: (0, 0)), pl.BlockSpec((1, dh), lambda i: (0, 0))],
        out_specs=[ospec, ospec, cspec, cspec],
        out_shape=[jax.ShapeDtypeStruct((m, width), BF16),
                   jax.ShapeDtypeStruct((m, width), BF16),
                   jax.ShapeDtypeStruct((m * il, V7X_LANES), F32),
                   jax.ShapeDtypeStruct((m * il, V7X_LANES), F32)],
        compiler_params=_params(("parallel",), rows_blk * width * 48),
        name="diff_prep",
    )(z, z, z, g_q.reshape(1, dh), g_k.reshape(1, dh))


def _softmax_update(s, v, m_ref, l_ref, acc_ref):
    m_old = m_ref[...]
    m_new = jnp.maximum(m_old, jnp.max(s, axis=-1, keepdims=True))
    alpha = jnp.exp(m_old - m_new)
    p = jnp.exp(s - m_new)
    l_ref[...] = alpha * l_ref[...] + jnp.sum(p, axis=-1, keepdims=True)
    acc_ref[...] = alpha * acc_ref[...] + jnp.dot(p.astype(BF16), v, preferred_element_type=F32)
    m_ref[...] = m_new


def _diff_attn_prompt_kernel(qi_tab, ki_tab, slope_ref, lam_ref, q_ref, k_ref, v_ref, gain_ref, y_ref,
                             m_s, l_s, acc_s, *, bq, bk, dh, out_scale, rows_sub):
    h = pl.program_id(1)
    pair = pl.program_id(2)
    qi = qi_tab[pair]
    ki = ki_tab[pair]

    @pl.when(ki == 0)
    def _():
        m_s[...] = jnp.full(m_s.shape, NEG, F32)
        l_s[...] = jnp.zeros(l_s.shape, F32)
        acc_s[...] = jnp.zeros(acc_s.shape, F32)

    def block_update(diagonal):
        rel = lax.broadcasted_iota(jnp.int32, (1, bk), 1) + (ki * bk - qi * bq)
        bias = slope_ref[h] * rel.astype(F32)
        for r in range(bq // rows_sub):
            rows = pl.ds(r * rows_sub, rows_sub)
            nk = min(bk, (r + 1) * rows_sub) if diagonal else bk
            if diagonal:
                qrow = lax.broadcasted_iota(jnp.int32, (rows_sub, 1), 0) + r * rows_sub
                row_bias = jnp.where(rel[:, :nk] <= qrow, bias[:, :nk], NEG)
            else:
                row_bias = bias
            for j in range(2):
                s = _nt_dot(q_ref[rows, j * dh:(j + 1) * dh], k_ref[:nk, j * dh:(j + 1) * dh]) + row_bias
                _softmax_update(s, v_ref[:nk, :], m_s.at[j, rows], l_s.at[j, rows], acc_s.at[j, rows])

    @pl.when(ki < qi)
    def _():
        block_update(False)

    @pl.when(ki == qi)
    def _():
        block_update(True)
        o = acc_s[0] / l_s[0] - lam_ref[0] * (acc_s[1] / l_s[1])
        y_ref[...] = (_group_rms(o, gain_ref[...], o.shape[1]) * out_scale).astype(y_ref.dtype)


def diff_attention_prompt(qn, kn, z, slopes, lam, gain, *, batch, seq, heads, dh, dv, v_off, blk, out_scale):
    bq = bk = min(blk, seq)
    nq = seq // bq
    hw = 2 * dh
    assert hw == dv and v_off % dv == 0
    vblk = v_off // dv
    pairs = [(qi, ki) for qi in range(nq) for ki in range(qi + 1)]
    qi_tab = jnp.asarray([p[0] for p in pairs], jnp.int32)
    ki_tab = jnp.asarray([p[1] for p in pairs], jnp.int32)

    grid_spec = pltpu.PrefetchScalarGridSpec(
        num_scalar_prefetch=2,
        grid=(batch, heads, len(pairs)),
        in_specs=[pl.BlockSpec(memory_space=pltpu.SMEM),
                  pl.BlockSpec(memory_space=pltpu.SMEM),
                  pl.BlockSpec((bq, hw), lambda b, h, p, qt, kt: (b * nq + qt[p], h)),
                  pl.BlockSpec((bk, hw), lambda b, h, p, qt, kt: (b * nq + kt[p], h)),
                  pl.BlockSpec((bk, dv), lambda b, h, p, qt, kt: (b * nq + kt[p], vblk + h)),
                  pl.BlockSpec((1, dv), lambda b, h, p, qt, kt: (0, h))],
        out_specs=pl.BlockSpec((bq, dv), lambda b, h, p, qt, kt: (b * nq + qt[p], h)),
        scratch_shapes=[pltpu.VMEM((2, bq, 1), F32), pltpu.VMEM((2, bq, 1), F32), pltpu.VMEM((2, bq, dv), F32)],
    )
    return pl.pallas_call(
        functools.partial(_diff_attn_prompt_kernel, bq=bq, bk=bk, dh=dh, out_scale=out_scale,
                          rows_sub=min(256, bq)),
        grid_spec=grid_spec,
        out_shape=jax.ShapeDtypeStruct((batch * seq, heads * dv), BF16),
        compiler_params=_params(("parallel", "parallel", "arbitrary"), 24 << 20),
        name="diff_attn_prompt",
    )(qi_tab, ki_tab, slopes, lam, qn, kn, z, gain.reshape(1, heads * dv))


def _diff_attn_decode_kernel(pt_ref, slope_ref, lam_ref, q_ref, kn_ref, vn_ref, gain_ref, *rest,
                             pages, page, past, valid, heads, dh, dv, out_scale):
    k_refs = rest[:pages]
    v_refs = rest[pages:2 * pages]
    y_ref, m_s, l_s, acc_s = rest[2 * pages:]
    step = pl.program_id(1)
    nsteps = pl.num_programs(1)
    rows = q_ref.shape[0]
    hw = 2 * dh

    @pl.when(step == 0)
    def _():
        m_s[...] = jnp.full(m_s.shape, NEG, F32)
        l_s[...] = jnp.zeros(l_s.shape, F32)
        acc_s[...] = jnp.zeros(acc_s.shape, F32)

    def stacked_queries(h):
        qh = q_ref[:, h * hw:(h + 1) * hw]
        zero = jnp.zeros((rows, dh), qh.dtype)
        return jnp.concatenate([jnp.concatenate([qh[:, :dh], zero], axis=1),
                                jnp.concatenate([zero, qh[:, dh:]], axis=1)], axis=0)

    nkeys = pages * page
    rel = lax.broadcasted_iota(jnp.int32, (1, nkeys), 1) + (step * nkeys - past)
    for h in range(heads):
        qs = stacked_queries(h)
        k = jnp.concatenate([_load_cache_head(r, h, heads, page, hw) for r in k_refs], axis=0).astype(BF16)
        v = jnp.concatenate([_load_cache_head(r, h, heads, page, dv) for r in v_refs], axis=0).astype(BF16)
        s = _nt_dot(qs, k) + slope_ref[h] * rel.astype(F32)
        _softmax_update(s, v, m_s.at[h], l_s.at[h], acc_s.at[h])

    @pl.when(step == nsteps - 1)
    def _():
        nk = V7X_LANES
        key = lax.broadcasted_iota(jnp.int32, (1, nk), 1)
        tok = lax.broadcasted_iota(jnp.int32, (2 * rows, 1), 0) & (rows - 1)
        ok = (key <= tok) & (key < valid)
        for h in range(heads):
            qs = stacked_queries(h)
            k = _pad_rows(kn_ref[:, h * hw:(h + 1) * hw], nk)
            v = _pad_rows(vn_ref[:, h * dv:(h + 1) * dv], nk)
            s = _nt_dot(qs, k) + slope_ref[h] * key.astype(F32)
            s = jnp.where(ok, s, NEG)
            _softmax_update(s, v, m_s.at[h], l_s.at[h], acc_s.at[h])
            o = acc_s[h] / l_s[h]
            o = o[:rows] - lam_ref[0] * o[rows:]
            y_ref[:, h * dv:(h + 1) * dv] = (
                _group_rms(o, gain_ref[:, h * dv:(h + 1) * dv], dv) * out_scale).astype(y_ref.dtype)


def diff_attention_decode(qn, kn, z, cache_k, cache_v, page_table, layer, slopes, lam, gain,
                          *, batch, rows_per_seq, valid_rows, heads, dh, dv, v_off, out_scale):
    depth, n_pool, page = cache_k.shape[:3]
    n_pages = page_table.shape[1]
    pages = PAGES_PER_STEP
    assert n_pages % pages == 0
    nsteps = n_pages // pages
    width = heads * dv
    ck = _cache_view(cache_k, depth * n_pool)
    cv = _cache_view(cache_v, depth * n_pool)
    base = layer * n_pool
    page_rows = page * width // V7X_LANES

    def page_spec(p):
        return pl.BlockSpec((1, page_rows, V7X_LANES),
                            lambda b, s, pt, p=p: (base + pt[b * n_pages + s * pages + p], 0, 0))

    row_spec = pl.BlockSpec((rows_per_seq, width), lambda b, s, pt: (b, 0))
    grid_spec = pltpu.PrefetchScalarGridSpec(
        num_scalar_prefetch=1,
        grid=(batch, nsteps),
        in_specs=[pl.BlockSpec(memory_space=pltpu.SMEM),
                  pl.BlockSpec(memory_space=pltpu.SMEM),
                  row_spec, row_spec,
                  pl.BlockSpec((rows_per_seq, width), lambda b, s, pt: (b, v_off // width)),
                  pl.BlockSpec((1, width), lambda b, s, pt: (0, 0))]
                 + [page_spec(p) for p in range(pages)] + [page_spec(p) for p in range(pages)],
        out_specs=row_spec,
        scratch_shapes=[pltpu.VMEM((heads, 2 * rows_per_seq, 1), F32),
                        pltpu.VMEM((heads, 2 * rows_per_seq, 1), F32),
                        pltpu.VMEM((heads, 2 * rows_per_seq, dv), F32)],
    )
    assert v_off % width == 0
    return pl.pallas_call(
        functools.partial(_diff_attn_decode_kernel, pages=pages, page=page, past=n_pages * page,
                          valid=valid_rows, heads=heads, dh=dh, dv=dv, out_scale=out_scale),
        grid_spec=grid_spec,
        out_shape=jax.ShapeDtypeStruct((batch * rows_per_seq, width), BF16),
        compiler_params=_params(("parallel", "arbitrary"), 4 * pages * page * width * 4 + (8 << 20)),
        name="diff_attn_decode",
    )(page_table.reshape(-1), slopes, lam, qn, kn, z, gain.reshape(1, width),
      *([ck] * pages), *([cv] * pages))


def _cross_attn_kernel(q_ref, mk_ref, mv_ref, o_ref, *, heads, dh, n_mem, cache_layout):
    outs = []
    for h in range(heads):
        q = q_ref[:, h * dh:(h + 1) * dh]
        if cache_layout:
            mk = _load_cache_head(mk_ref, h, heads, n_mem, dh).astype(BF16)
            mv = _load_cache_head(mv_ref, h, heads, n_mem, dh).astype(BF16)
        else:
            mk = mk_ref[:, h * dh:(h + 1) * dh].astype(BF16)
            mv = mv_ref[:, h * dh:(h + 1) * dh].astype(BF16)
        s = _nt_dot(q, mk) * dh ** -0.5
        p = jnp.exp(s - jnp.max(s, axis=-1, keepdims=True))
        o = jnp.dot(p.astype(BF16), mv, preferred_element_type=F32)
        outs.append(o / jnp.sum(p, axis=-1, keepdims=True))
    o_ref[...] = jnp.concatenate(outs, axis=1).astype(o_ref.dtype)


def cross_attention(qx, mk, mv, *, batch, rows_per_seq, n_mem, heads, dh, rows_blk, cache_layer=None):
    width = heads * dh
    rows_blk = min(rows_blk, rows_per_seq)
    nq = rows_per_seq // rows_blk
    if cache_layer is None:
        mem_spec = pl.BlockSpec((n_mem, width), lambda b, i: (b, 0))
    else:
        lead = mk.shape[0] * mk.shape[1]
        mk, mv = _cache_view(mk, lead), _cache_view(mv, lead)
        mem_spec = pl.BlockSpec((1, n_mem * width // V7X_LANES, V7X_LANES),
                                lambda b, i: (cache_layer * batch + b, 0, 0))
    return pl.pallas_call(
        functools.partial(_cross_attn_kernel, heads=heads, dh=dh, n_mem=n_mem,
                          cache_layout=cache_layer is not None),
        grid=(batch, nq),
        in_specs=[pl.BlockSpec((rows_blk, width), lambda b, i: (b * nq + i, 0)), mem_spec, mem_spec],
        out_specs=pl.BlockSpec((rows_blk, width), lambda b, i: (b * nq + i, 0)),
        out_shape=jax.ShapeDtypeStruct((batch * rows_per_seq, width), BF16),
        compiler_params=_params(("parallel", "parallel"), 24 << 20),
        name="cross_attn",
    )(qx, mk, mv)


def kernel(x_prompt, x_sample, cache_diff_k, cache_diff_v, cache_mem_k, cache_mem_v, state_conv, state_mlstm_C, state_mlstm_n, state_mlstm_m, state_ret, page_table, mem_prompt, norm_mix, w_in, conv_w, b_igate, b_fgate, g_mlstm, g_diff_q, g_diff_k, lam_q1, lam_k1, lam_q2, lam_k2, g_diff_out, g_ret, w_out, norm_xattn, norm_mem, w_xq, w_xk, w_xv, g_xq, g_xk, w_xo, norm_mlp, w_up, w_down):
    depth = w_in.shape[0]
    bp, seq, d_model = x_prompt.shape
    bs, dec = x_sample.shape[:2]
    conv_ch = conv_w.shape[2]
    _, _, hb, dkb, dvb = state_mlstm_C.shape
    _, _, hd, dkd, dvd = state_ret.shape
    hc, dvc = cache_diff_v.shape[3:]
    dhc = dvc // 2
    _, _, n_mem, hx, dhx = cache_mem_k.shape
    xw = hx * dhx
    assert seq % CHUNK_ROWS == 0 and dec <= SAMPLE_ROWS

    sizes = (conv_ch, conv_ch, conv_ch, hb * dkb, hb * dkb, hb * dvb, hb * dvb, hb, hb,
             2 * hc * dhc, 2 * hc * dhc, hc * dvc, hd * dkd, hd * dkd, hd * dvd, hd * dvd)
    names = ("a_x", "a_b", "a_c", "q_b", "k_b", "v_b", "o_b", "i_b", "f_b",
             "q_c", "k_c", "v_c", "q_d", "k_d", "v_d", "g_d")
    off, pos = {}, 0
    for nm, sz in zip(names, sizes):
        if nm not in ("i_b", "f_b"):
            off[nm] = pos
            pos += sz

    slopes = (2.0 ** (-8.0 * jnp.arange(1, hc + 1, dtype=F32) / hc)).astype(F32)

    x_p = x_prompt.reshape(bp * seq, d_model)
    x_s = jnp.pad(x_sample, ((0, 0), (0, SAMPLE_ROWS - dec), (0, 0))).reshape(bs * SAMPLE_ROWS, d_model)
    mem_rows = mem_prompt.reshape(bp * n_mem, d_model)

    groups = (
        dict(batch=bp, rows=seq, valid=seq, tm=1024, prompt=True),
        dict(batch=bs, rows=SAMPLE_ROWS, valid=dec, tm=bs * SAMPLE_ROWS, prompt=False),
    )
    xs = [x_p, x_s]
    outs = [dict(conv=[], C=[], n=[], m=[], k=[], v=[], S=[], mk=[], mv=[]) for _ in groups]

    gate_lo = sum(sizes[:7])
    w_main, w_gate = split_w_in(w_in, gate_lo, gate_lo + 2 * hb)
    w_out_b, w_xq_b, w_xk_b, w_xv_b, w_xo_b = (cast_bf16(w) for w in (w_out, w_xq, w_xk, w_xv, w_xo))

    for li in range(depth):
        lam_init = 0.8 - 0.6 * math.exp(-0.3 * li)
        lam = (jnp.exp(jnp.sum(lam_q1[li] * lam_k1[li])) - jnp.exp(jnp.sum(lam_q2[li] * lam_k2[li]))
               + lam_init).reshape(1).astype(F32)
        gate_bias = jnp.pad(jnp.concatenate([b_igate[li], b_fgate[li]]), (0, V7X_LANES - 2 * hb)).reshape(1, V7X_LANES)

        hm = rmsnorm_rows(mem_rows, norm_mem[li], 256)
        mk_p = matmul([hm], w_xk_b, li, tm=1024, tn=1024, out_dtype=F32,
                      epilogue="headnorm", extra=g_xk[li], group=dhx, name="mm_mem_k")
        mv_p = matmul([hm], w_xv_b, li, tm=1024, tn=1024, out_dtype=F32, name="mm_mem_v")

        for gi, g in enumerate(groups):
            x = xs[gi]
            batch, rows, valid, tm = g["batch"], g["rows"], g["valid"], g["tm"]
            if g["prompt"]:
                conv_buf = jnp.zeros((batch, 2, conv_ch), F32)
                c0 = jnp.zeros((batch, hb, dkb, dvb), F32)
                n0 = jnp.zeros((batch, hb, dkb), F32)
                m0 = jnp.zeros((batch, hb), F32)
                s0 = jnp.zeros((batch, hd, dkd, dvd), F32)
                mk, mv, cache_layer = mk_p, mv_p, None
            else:
                conv_buf, c0, n0, m0, s0 = (state_conv[li], state_mlstm_C[li], state_mlstm_n[li],
                                            state_mlstm_m[li], state_ret[li])
                mk, mv, cache_layer = cache_mem_k, cache_mem_v, li

            h = rmsnorm_rows(x, norm_mix[li], 512)
            if g["prompt"]:
                z, w_up_l = matmul([h], w_main, li, tm=tm, tn=1024, side=(w_up, li), name="mm_in")
            else:
                z = matmul([h], w_main, li, tm=tm, tn=1024, name="mm_in")
            gates = matmul([h], w_gate, li, tm=tm, tn=V7X_LANES, out_dtype=F32, name="mm_gate")
            y_a, conv_new = short_conv(z, conv_buf, conv_w[li], batch=batch, rows_per_seq=rows, rows_blk=512,
                                       valid_rows=valid,
                                       col_blocks=(off["a_x"] // conv_ch, off["a_b"] // conv_ch, off["a_c"] // conv_ch))
            y_b, c_new, n_new, m_new = mlstm_mixer(
                z, gates, gate_bias, g_mlstm[li], c0, n0, m0, batch=batch, rows_per_seq=rows, valid_rows=valid,
                cols=(off["q_b"], off["k_b"], off["v_b"], off["o_b"]))
            qn, kb, kc, vc = diff_prep(z, g_diff_q[li], g_diff_k[li], rows_blk=512,
                                       cols=(off["q_c"], off["k_c"], off["v_c"]), width=hc * dvc, heads=hc)
            if g["prompt"]:
                y_c = diff_attention_prompt(qn, kb, z, slopes, lam, g_diff_out[li], batch=batch, seq=rows,
                                            heads=hc, dh=dhc, dv=dvc, v_off=off["v_c"], blk=1024,
                                            out_scale=1.0 - lam_init)
            else:
                y_c = diff_attention_decode(qn, kb, z, cache_diff_k, cache_diff_v, page_table, li, slopes, lam,
                                            g_diff_out[li], batch=batch, rows_per_seq=rows, valid_rows=valid,
                                            heads=hc, dh=dhc, dv=dvc, v_off=off["v_c"], out_scale=1.0 - lam_init)
            y_d, s_new = retention_mixer(z, g_ret[li], s0, batch=batch, rows_per_seq=rows, valid_rows=valid,
                                         cols=(off["q_d"], off["k_d"], off["v_d"], off["g_d"]))
            x, xg2, ssq2 = matmul([y_a, y_b, y_c, y_d], w_out_b, li, tm=tm, tn=512, out_dtype=F32,
                                  epilogue="residual", extra=x, next_gain=norm_xattn[li], name="mm_out")

            qx = matmul([xg2], w_xq_b, li, tm=tm, tn=1024, epilogue="headnorm", extra=g_xq[li], group=dhx,
                        row_ssq=ssq2, name="mm_xq")
            ox = cross_attention(qx, mk, mv, batch=batch, rows_per_seq=rows, n_mem=n_mem, heads=hx, dh=dhx,
                                 rows_blk=512, cache_layer=cache_layer)
            x, xg3, ssq3 = matmul([ox], w_xo_b, li, tm=tm, tn=1024, out_dtype=F32, epilogue="residual", extra=x,
                                  next_gain=norm_mlp[li], name="mm_xo")

            if g["prompt"]:
                hid, w_down_l = matmul([xg3], w_up_l, 0, tm=tm, tn=1024, epilogue="relu2", row_ssq=ssq3,
                                       side=(w_down, li), name="mm_up")
            else:
                hid = matmul([xg3], w_up_l, 0, tm=tm, tn=1024, epilogue="relu2", row_ssq=ssq3, name="mm_up")
            x = matmul([hid], w_down_l, 0, tm=tm, tn=1024, tk=4096, out_dtype=F32,
                       epilogue="residual", extra=x, name="mm_down")

            xs[gi] = x
            o = outs[gi]
            o["conv"].append(conv_new); o["C"].append(c_new); o["n"].append(n_new); o["m"].append(m_new)
            o["k"].append(kc); o["v"].append(vc); o["S"].append(s_new)
            if g["prompt"]:
                o["mk"].append(mk_p); o["mv"].append(mv_p)

    op, os_ = outs
    y_prompt = xs[0].reshape(bp, seq, d_model)
    y_sample = xs[1].reshape(bs, SAMPLE_ROWS, d_model)[:, :dec]

    def cache_rows(parts, batch, rows, valid):
        a = jnp.stack([_cache_unview(p, batch, rows, hc, dvc) for p in parts])
        return a if rows == valid else a[:, :, :valid]

    return (y_prompt, y_sample,
            jnp.stack(op["conv"]), jnp.stack(op["C"]), jnp.stack(op["n"]), jnp.stack(op["m"]),
            cache_rows(op["k"], bp, seq, seq), cache_rows(op["v"], bp, seq, seq), jnp.stack(op["S"]),
            jnp.stack(op["mk"]).reshape(depth, bp, n_mem, hx, dhx),
            jnp.stack(op["mv"]).reshape(depth, bp, n_mem, hx, dhx),
            jnp.stack(os_["conv"]), jnp.stack(os_["C"]), jnp.stack(os_["n"]), jnp.stack(os_["m"]),
            cache_rows(os_["k"], bs, SAMPLE_ROWS, dec), cache_rows(os_["v"], bs, SAMPLE_ROWS, dec),
            jnp.stack(os_["S"]))
```
